```python
import math
import jax, jax.numpy as jnp
from jax import lax
import numpy as np


D_MODEL = 1024
BATCH = 16
SEQ = 2048
DEPTH = 1

CHUNK = 64
Q_BLOCK = 128
MLA_HEADS = 8
QK_NOPE_DIM = 64
QK_ROPE_DIM = 32
V_HEAD_DIM = 64
Q_LORA_RANK = 256
KV_LORA_RANK = 128
ATTN_WIDTH = MLA_HEADS * V_HEAD_DIM
CONV_WIDTH = D_MODEL - ATTN_WIDTH
CONV_GROUPS = 8
SHORT_CONV_K = 3
D_FF = 2816
FFN_CONV_K = 3
ROPE_THETA = 10000.0
RMS_EPS = 1e-6
LN_EPS = 1e-5
DEEPNORM_ALPHA = (2.0 * DEPTH) ** 0.25
DEEPNORM_BETA = (8.0 * DEPTH) ** -0.25
IN_PROJ_WIDTH = Q_LORA_RANK + KV_LORA_RANK + QK_ROPE_DIM + 3 * CONV_WIDTH

kernel_name = 'hybrid_mla_shortconv_convffn_block'


def layer_norm(x, g=None, b=None):
    xf = x.astype(jnp.float32)
    mu = jnp.mean(xf, axis=-1, keepdims=True)
    var = jnp.mean(jnp.square(xf - mu), axis=-1, keepdims=True)
    y = (xf - mu) * lax.rsqrt(var + LN_EPS)
    if g is not None:
        y = y * g.astype(jnp.float32) + b.astype(jnp.float32)
    return y.astype(x.dtype)


def rms_norm(x, g):
    xf = x.astype(jnp.float32)
    y = xf * lax.rsqrt(jnp.mean(jnp.square(xf), axis=-1, keepdims=True) + RMS_EPS)
    return (y * g.astype(jnp.float32)).astype(x.dtype)


def group_rms_norm(y, n_groups, g):
    B, S, W = y.shape
    yf = y.reshape(B, S, n_groups, W // n_groups).astype(jnp.float32)
    yf = yf * lax.rsqrt(jnp.mean(jnp.square(yf), axis=-1, keepdims=True) + RMS_EPS)
    return (yf.reshape(B, S, W) * g.astype(jnp.float32)).astype(y.dtype)


def rope_cos_sin(positions, dim, dtype):
    inv_freq = ROPE_THETA ** (-jnp.arange(0, dim, 2, dtype=jnp.float32) / dim)
    ang = positions.astype(jnp.float32)[..., None] * inv_freq
    return jnp.cos(ang).astype(dtype), jnp.sin(ang).astype(dtype)


def apply_rope(x, cos, sin):
    x1, x2 = jnp.split(x, 2, axis=-1)
    return jnp.concatenate([x1 * cos - x2 * sin, x2 * cos + x1 * sin], axis=-1)


def causal_dwconv(u, w, b):
    K = w.shape[0]
    S = u.shape[1]
    up = jnp.pad(u, ((0, 0), (K - 1, 0), (0, 0)))
    y = b
    for k in range(K):
        y = y + w[k] * up[:, k:k + S]
    return y


def chunk_causal_mla_attention(q_nope, q_rope, k_nope, k_rope, v):
    B, S, H, _ = q_nope.shape
    nb = S // Q_BLOCK
    scale = (QK_NOPE_DIM + QK_ROPE_DIM) ** -0.5
    key_chunk = jnp.arange(S) // CHUNK

    def to_blocks(t):
        return t.reshape((B, nb, Q_BLOCK) + t.shape[2:]).swapaxes(0, 1)

    def block(args):
        qn, qr, bi = args
        s = (jnp.einsum('bqhd,bkhd->bhqk', qn, k_nope)
             + jnp.einsum('bqhr,bkr->bhqk', qr, k_rope)).astype(jnp.float32) * scale
        q_chunk = (bi * Q_BLOCK + jnp.arange(Q_BLOCK)) // CHUNK
        allowed = key_chunk[None, :] <= q_chunk[:, None]
        s = jnp.where(allowed[None, None], s, -1e30)
        p = jax.nn.softmax(s, axis=-1).astype(v.dtype)
        return jnp.einsum('bhqk,bkhd->bqhd', p, v)

    out = lax.map(block, (to_blocks(q_nope), to_blocks(q_rope), jnp.arange(nb)))
    return out.swapaxes(0, 1).reshape(B, S, H * V_HEAD_DIM)


def hybrid_mixer(h, cos, sin, w_in, q_norm_g, w_q_up, kv_norm_g, w_kv_up,
                 conv_w, conv_b, out_norm_g, w_out):
    B, S, _ = h.shape
    proj = h @ w_in
    splits = np.cumsum([Q_LORA_RANK, KV_LORA_RANK, QK_ROPE_DIM, CONV_WIDTH, CONV_WIDTH]).tolist()
    c_q, c_kv, k_rope_raw, gate_b, gate_c, conv_v = jnp.split(proj, splits, axis=-1)

    q = (rms_norm(c_q, q_norm_g) @ w_q_up).reshape(B, S, MLA_HEADS, QK_NOPE_DIM + QK_ROPE_DIM)
    q_nope, q_rope = q[..., :QK_NOPE_DIM], q[..., QK_NOPE_DIM:]
    q_rope = apply_rope(q_rope, cos[:, :, None, :], sin[:, :, None, :])
    kv = (rms_norm(c_kv, kv_norm_g) @ w_kv_up).reshape(B, S, MLA_HEADS, QK_NOPE_DIM + V_HEAD_DIM)
    k_nope, v = kv[..., :QK_NOPE_DIM], kv[..., QK_NOPE_DIM:]
    k_rope = apply_rope(k_rope_raw, cos, sin)
    y_attn = chunk_causal_mla_attention(q_nope, q_rope, k_nope, k_rope, v)

    y_conv = gate_b * causal_dwconv(gate_c * conv_v, conv_w, conv_b)

    y = jnp.concatenate([group_rms_norm(y_attn, MLA_HEADS, out_norm_g[:ATTN_WIDTH]),
                         group_rms_norm(y_conv, CONV_GROUPS, out_norm_g[ATTN_WIDTH:])], axis=-1)
    return y @ w_out


def conv_ffn(h, w_up, ffn_conv_w, ffn_conv_b, w_down):
    u = causal_dwconv(h @ w_up, ffn_conv_w, ffn_conv_b)
    g, val = jnp.split(u, 2, axis=-1)
    return (jax.nn.silu(g) * val) @ w_down


def setup_inputs(seed: int = 0) -> dict:
    key = jax.random.key(seed)
    ks = jax.random.split(key, 24)
    L, D = DEPTH, D_MODEL

    def nrm(k, shape, scale):
        return jax.random.normal(k, shape, jnp.float32) * scale

    def gain(k, shape):
        return 1.0 + 0.02 * jax.random.normal(k, shape, jnp.float32)

    offsets = jax.random.randint(ks[2], (BATCH,), 0, 4096, dtype=jnp.int32)
    positions = offsets[:, None] + jnp.arange(SEQ, dtype=jnp.int32)[None, :]
    return {
        'x': nrm(ks[0], (BATCH, SEQ, D), 1.0),
        'c': nrm(ks[1], (BATCH, D), 1.0),
        'positions': positions,
        'w_ada': nrm(ks[3], (L, D, 6 * D), D ** -0.5),
        'b_ada': nrm(ks[4], (L, 6 * D), 0.02),
        'w_in': nrm(ks[5], (L, D, IN_PROJ_WIDTH), D ** -0.5),
        'q_norm_g': gain(ks[6], (L, Q_LORA_RANK)),
        'w_q_up': nrm(ks[7], (L, Q_LORA_RANK, MLA_HEADS * (QK_NOPE_DIM + QK_ROPE_DIM)), Q_LORA_RANK ** -0.5),
        'kv_norm_g': gain(ks[8], (L, KV_LORA_RANK)),
        'w_kv_up': nrm(ks[9], (L, KV_LORA_RANK, MLA_HEADS * (QK_NOPE_DIM + V_HEAD_DIM)), KV_LORA_RANK ** -0.5),
        'conv_w': nrm(ks[10], (L, SHORT_CONV_K, CONV_WIDTH), SHORT_CONV_K ** -0.5),
        'conv_b': nrm(ks[11], (L, CONV_WIDTH), 0.02),
        'out_norm_g': gain(ks[12], (L, D)),
        'w_out': nrm(ks[13], (L, D, D), D ** -0.5 * DEEPNORM_BETA),
        'ln1_g': gain(ks[14], (L, D)),
        'ln1_b': nrm(ks[15], (L, D), 0.02),
        'w_up': nrm(ks[16], (L, D, 2 * D_FF), D ** -0.5),
        'ffn_conv_w': nrm(ks[17], (L, FFN_CONV_K, 2 * D_FF), FFN_CONV_K ** -0.5),
        'ffn_conv_b': nrm(ks[18], (L, 2 * D_FF), 0.02),
        'w_down': nrm(ks[19], (L, D_FF, D), D_FF ** -0.5 * DEEPNORM_BETA),
        'ln2_g': gain(ks[20], (L, D)),
        'ln2_b': nrm(ks[21], (L, D), 0.02),
    }


def reference(x, c, positions, w_ada, b_ada, w_in, q_norm_g, w_q_up, kv_norm_g, w_kv_up,
              conv_w, conv_b, out_norm_g, w_out, ln1_g, ln1_b, w_up, ffn_conv_w, ffn_conv_b,
              w_down, ln2_g, ln2_b):
    cos, sin = rope_cos_sin(positions, QK_ROPE_DIM, x.dtype)
    c_act = jax.nn.silu(c)
    for l in range(DEPTH):
        mod = c_act @ w_ada[l] + b_ada[l]
        shift_m, scale_m, gate_m, shift_f, scale_f, gate_f = [m[:, None, :] for m in jnp.split(mod, 6, axis=-1)]
        h = layer_norm(x) * (1.0 + scale_m) + shift_m
        mix = hybrid_mixer(h, cos, sin, w_in[l], q_norm_g[l], w_q_up[l], kv_norm_g[l], w_kv_up[l],
                           conv_w[l], conv_b[l], out_norm_g[l], w_out[l])
        x = layer_norm(DEEPNORM_ALPHA * x + gate_m * mix, ln1_g[l], ln1_b[l])
        h = layer_norm(x) * (1.0 + scale_f) + shift_f
        ff = conv_ffn(h, w_up[l], ffn_conv_w[l], ffn_conv_b[l], w_down[l])
        x = layer_norm(DEEPNORM_ALPHA * x + gate_f * ff, ln2_g[l], ln2_b[l])
    return x
```

```python
import functools

import jax
import jax.numpy as jnp
from jax import lax
from jax.experimental import pallas as pl
from jax.experimental.pallas import tpu as pltpu

D_MODEL = 1024
SEQ = 2048
CHUNK = 64
MLA_HEADS = 8
QK_NOPE_DIM = 64
QK_ROPE_DIM = 32
V_HEAD_DIM = 64
Q_LORA_RANK = 256
KV_LORA_RANK = 128
ATTN_WIDTH = MLA_HEADS * V_HEAD_DIM
CONV_WIDTH = D_MODEL - ATTN_WIDTH
CONV_GROUPS = 8
D_FF = 2816
ROPE_THETA = 10000.0
RMS_EPS = 1e-6
LN_EPS = 1e-5
DEEPNORM_ALPHA = 2.0 ** 0.25

LANES = 128
HEAD_LANES = 128
IN_EXT = Q_LORA_RANK + KV_LORA_RANK + LANES + 3 * CONV_WIDTH
TOK_TILE = 512
Q_TILE = 256
K_TILE = 256
FF_CHUNK = 256
N_FF_CHUNKS = D_FF // FF_CHUNK
NEG = -1e30
Q_SCALE = (QK_NOPE_DIM + QK_ROPE_DIM) ** -0.5 * 1.4426950408889634

F32 = jnp.float32
BF16 = jnp.bfloat16


def _layer_norm(x):
    mu = jnp.mean(x, axis=-1, keepdims=True)
    xc = x - mu
    var = jnp.mean(xc * xc, axis=-1, keepdims=True)
    return xc * lax.rsqrt(var + LN_EPS)


def _rms(x):
    return x * lax.rsqrt(jnp.mean(x * x, axis=-1, keepdims=True) + RMS_EPS)


def _sigmoid(x):
    return 1.0 / (1.0 + jnp.exp(-x))


def _rope_kernel(pos_ref, freq_ref, cos_ref, sin_ref):
    ang = pos_ref[...] * freq_ref[...]
    cos_ref[...] = jnp.cos(ang)
    sin_ref[...] = jnp.sin(ang)


def _rope_table(positions):
    b, s = positions.shape
    half = QK_ROPE_DIM // 2
    rows = b * s * half // LANES
    inv_freq = ROPE_THETA ** (-jnp.arange(0, QK_ROPE_DIM, 2, dtype=F32) / QK_ROPE_DIM)
    pos = jnp.repeat(positions.astype(F32), half, axis=-1).reshape(rows, LANES)
    freq = jnp.tile(inv_freq, LANES // half).reshape(1, LANES)
    cos, sin = pl.pallas_call(
        _rope_kernel,
        out_shape=(jax.ShapeDtypeStruct((rows, LANES), F32),) * 2,
        name="rope_table",
    )(pos, freq)
    cos = cos.reshape(b, s, half)
    sin = sin.reshape(b, s, half)
    return jnp.concatenate([cos, cos, -sin, sin, cos, cos, -sin, sin], axis=-1)


def _mod_kernel(c_ref, w_ref, b_ref, o_ref):
    c = c_ref[...]
    act = c * _sigmoid(c)
    o_ref[...] = jnp.dot(act, w_ref[...], preferred_element_type=F32,
                         precision=lax.Precision.HIGHEST) + b_ref[...]


def _adaln_mod(c, w_ada, b_ada):
    b, d = c.shape
    n = w_ada.shape[1]
    return pl.pallas_call(
        _mod_kernel,
        grid=(n // d,),
        in_specs=[pl.BlockSpec((b, d), lambda j: (0, 0)),
                  pl.BlockSpec((d, d), lambda j: (0, j)),
                  pl.BlockSpec((1, d), lambda j: (0, j))],
        out_specs=pl.BlockSpec((b, d), lambda j: (0, j)),
        out_shape=jax.ShapeDtypeStruct((b, n), F32),
        compiler_params=pltpu.CompilerParams(dimension_semantics=("arbitrary",),
                                             vmem_limit_bytes=32 * 1024 * 1024),
        name="adaln_mod",
    )(c, w_ada, b_ada.reshape(1, n))


def _proj_kernel(x_ref, mod_ref, tab_ref, win_ref, qg_ref, wq_ref, kvg_ref, wk_ref, wvt_ref,
                 cw_ref, cb_ref, og_ref, q_ref, k_ref, vt_ref, yc_ref, ubuf_ref):
    t = TOK_TILE
    si = pl.program_id(1)
    shift = mod_ref[0, 0:1, :]
    scale = mod_ref[0, 1:2, :]
    h = (_layer_norm(x_ref[0]) * (1.0 + scale) + shift).astype(BF16)
    proj = jnp.dot(h, win_ref[...], preferred_element_type=F32)

    tab = tab_ref[0]
    lane = lax.broadcasted_iota(jnp.int32, (1, LANES), 1)
    is_nope = lane < QK_NOPE_DIM

    cq = (_rms(proj[:, 0:Q_LORA_RANK]) * qg_ref[...]).astype(BF16)
    q = jnp.dot(cq, wq_ref[...], preferred_element_type=F32)
    q_tab = jnp.where(is_nope, Q_SCALE, Q_SCALE * tab)
    for hh in range(MLA_HEADS):
        sl = slice(hh * HEAD_LANES, (hh + 1) * HEAD_LANES)
        q_ref[0, :, sl] = (q[:, sl] * q_tab).astype(BF16)

    c0 = Q_LORA_RANK
    ckv = (_rms(proj[:, c0:c0 + KV_LORA_RANK]) * kvg_ref[...]).astype(BF16)
    knope = jnp.dot(ckv, wk_ref[...], preferred_element_type=F32)
    c1 = c0 + KV_LORA_RANK
    r = proj[:, c1:c1 + LANES] * tab
    kf = r + pltpu.roll(r, QK_ROPE_DIM, axis=1)
    kf = jnp.where(is_nope, 0.0, kf)
    for hh in range(MLA_HEADS):
        sl = slice(hh * HEAD_LANES, (hh + 1) * HEAD_LANES)
        k_ref[0, :, sl] = (knope[:, sl] + kf).astype(BF16)
    vt = lax.dot_general(wvt_ref[...], ckv, (((1,), (1,)), ((), ())),
                         preferred_element_type=F32).astype(BF16)
    for j in range(t // K_TILE):
        vt_ref[0, j] = vt[:, j * K_TILE:(j + 1) * K_TILE]

    c2 = c1 + LANES
    gate_b = proj[:, c2:c2 + CONV_WIDTH]
    u = proj[:, c2 + CONV_WIDTH:c2 + 2 * CONV_WIDTH] * proj[:, c2 + 2 * CONV_WIDTH:c2 + 3 * CONV_WIDTH]

    @pl.when(si == 0)
    def _():
        ubuf_ref[0:8, :] = jnp.zeros((8, CONV_WIDTH), F32)

    ubuf_ref[8:t + 8, :] = u
    u1 = ubuf_ref[7:t + 7, :]
    u2 = ubuf_ref[6:t + 6, :]
    ubuf_ref[0:8, :] = ubuf_ref[t:t + 8, :]
    y = cb_ref[...] + cw_ref[0:1, :] * u2 + cw_ref[1:2, :] * u1 + cw_ref[2:3, :] * u
    y = gate_b * y
    lo = lane < (CONV_WIDTH // CONV_GROUPS)
    for cc in range(CONV_WIDTH // LANES):
        sl = slice(cc * LANES, (cc + 1) * LANES)
        yv = y[:, sl]
        sq = yv * yv
        ms_lo = jnp.sum(jnp.where(lo, sq, 0.0), axis=-1, keepdims=True)
        ms_hi = jnp.sum(jnp.where(lo, 0.0, sq), axis=-1, keepdims=True)
        ms = jnp.where(lo, ms_lo, ms_hi) * (1.0 / (CONV_WIDTH // CONV_GROUPS))
        yc_ref[0, :, sl] = (yv * lax.rsqrt(ms + RMS_EPS) * og_ref[:, sl]).astype(BF16)


def _mixer_proj(x, mod, tab, win, qg, wq, kvg, wk, wvt, cw, cb, og):
    b, s, d = x.shape
    t = TOK_TILE
    nt = s // t
    const = lambda shape: pl.BlockSpec(shape, lambda bi, si: (0,) * len(shape))
    return pl.pallas_call(
        _proj_kernel,
        grid=(b, nt),
        in_specs=[pl.BlockSpec((1, t, d), lambda bi, si: (bi, si, 0)),
                  pl.BlockSpec((1, 6, d), lambda bi, si: (bi, 0, 0)),
                  pl.BlockSpec((1, t, LANES), lambda bi, si: (bi, si, 0)),
                  const(win.shape), const(qg.shape), const(wq.shape), const(kvg.shape),
                  const(wk.shape), const(wvt.shape), const(cw.shape), const(cb.shape), const(og.shape)],
        out_specs=[pl.BlockSpec((1, t, MLA_HEADS * HEAD_LANES), lambda bi, si: (bi, si, 0)),
                   pl.BlockSpec((1, t, MLA_HEADS * HEAD_LANES), lambda bi, si: (bi, si, 0)),
                   pl.BlockSpec((1, t // K_TILE, ATTN_WIDTH, K_TILE), lambda bi, si: (bi, si, 0, 0)),
                   pl.BlockSpec((1, t, CONV_WIDTH), lambda bi, si: (bi, si, 0))],
        out_shape=[jax.ShapeDtypeStruct((b, s, MLA_HEADS * HEAD_LANES), BF16),
                   jax.ShapeDtypeStruct((b, s, MLA_HEADS * HEAD_LANES), BF16),
                   jax.ShapeDtypeStruct((b, s // K_TILE, ATTN_WIDTH, K_TILE), BF16),
                   jax.ShapeDtypeStruct((b, s, CONV_WIDTH), BF16)],
        scratch_shapes=[pltpu.VMEM((t + 8, CONV_WIDTH), F32)],
        compiler_params=pltpu.CompilerParams(dimension_semantics=("arbitrary", "arbitrary"),
                                             vmem_limit_bytes=56 * 1024 * 1024),
        name="mixer_proj",
    )(x, mod, tab, win, qg, wq, kvg, wk, wvt, cw, cb, og)


def _attn_kernel(q_ref, k_ref, vt_ref, g_ref, o_ref, ot_ref):
    qi = pl.program_id(2)
    heads = 2
    qs = [q_ref[0, :, hh * HEAD_LANES:(hh + 1) * HEAD_LANES] for hh in range(heads)]
    key_chunk = lax.broadcasted_iota(jnp.int32, (K_TILE, Q_TILE), 0) // CHUNK
    qry_chunk = lax.broadcasted_iota(jnp.int32, (K_TILE, Q_TILE), 1) // CHUNK
    allowed = key_chunk <= qry_chunk

    def step(j, carry, masked):
        new = []
        start = pl.multiple_of(j * K_TILE, K_TILE)
        for hh in range(heads):
            m, l, acc = carry[hh]
            k = k_ref[0, pl.ds(start, K_TILE), hh * HEAD_LANES:(hh + 1) * HEAD_LANES]
            s = lax.dot_general(k, qs[hh], (((1,), (1,)), ((), ())),
                                preferred_element_type=F32)
            if masked:
                s = jnp.where(allowed, s, NEG)
            m_new = jnp.maximum(m, jnp.max(s, axis=0, keepdims=True))
            alpha = jnp.exp2(m - m_new)
            p = jnp.exp2(s - m_new)
            l_new = alpha * l + jnp.sum(p, axis=0, keepdims=True)
            vt = vt_ref[0, j, hh * V_HEAD_DIM:(hh + 1) * V_HEAD_DIM, :]
            acc_new = alpha * acc + jnp.dot(vt, p.astype(BF16), preferred_element_type=F32)
            new.append((m_new, l_new, acc_new))
        return tuple(new)

    init = tuple((jnp.full((1, Q_TILE), NEG, F32), jnp.zeros((1, Q_TILE), F32),
                  jnp.zeros((V_HEAD_DIM, Q_TILE), F32)) for _ in range(heads))
    carry = lax.fori_loop(0, qi, lambda j, c: step(j, c, False), init)
    carry = step(qi, carry, True)
    for hh in range(heads):
        _, l, acc = carry[hh]
        o = acc / l
        ms = jnp.mean(o * o, axis=0, keepdims=True)
        ot_ref[hh * V_HEAD_DIM:(hh + 1) * V_HEAD_DIM, :] = o * lax.rsqrt(ms + RMS_EPS)
    o_ref[0] = (ot_ref[...].T * g_ref[...]).astype(BF16)


def _attention(q, k, vt, g):
    b, s, _ = q.shape
    pairs = MLA_HEADS // 2
    return pl.pallas_call(
        _attn_kernel,
        grid=(b, pairs, s // Q_TILE),
        in_specs=[pl.BlockSpec((1, Q_TILE, 2 * HEAD_LANES), lambda bi, pi, qi: (bi, qi, pi)),
                  pl.BlockSpec((1, s, 2 * HEAD_LANES), lambda bi, pi, qi: (bi, 0, pi)),
                  pl.BlockSpec((1, s // K_TILE, 2 * V_HEAD_DIM, K_TILE), lambda bi, pi, qi: (bi, 0, pi, 0)),
                  pl.BlockSpec((1, 2 * V_HEAD_DIM), lambda bi, pi, qi: (0, pi))],
        out_specs=pl.BlockSpec((1, Q_TILE, 2 * V_HEAD_DIM), lambda bi, pi, qi: (bi, qi, pi)),
        out_shape=jax.ShapeDtypeStruct((b, s, ATTN_WIDTH), BF16),
        scratch_shapes=[pltpu.VMEM((2 * V_HEAD_DIM, Q_TILE), F32)],
        compiler_params=pltpu.CompilerParams(dimension_semantics=("arbitrary",) * 3,
                                             vmem_limit_bytes=32 * 1024 * 1024),
        name="attention",
    )(q, k, vt, g)


def _ffn_kernel(x_ref, ya_ref, yc_ref, mod_ref, woa_ref, woc_ref, ln1g_ref, ln1b_ref, wup_ref,
                fcw_ref, fcb_ref, wdn_ref, ln2g_ref, ln2b_ref, o_ref, ubuf_ref, carry_ref, act_ref):
    t = TOK_TILE
    si = pl.program_id(1)
    gate_m = mod_ref[0, 2:3, :]
    shift_f = mod_ref[0, 3:4, :]
    scale_f = mod_ref[0, 4:5, :]
    gate_f = mod_ref[0, 5:6, :]

    mix = (jnp.dot(ya_ref[0], woa_ref[...], preferred_element_type=F32)
           + jnp.dot(yc_ref[0], woc_ref[...], preferred_element_type=F32))
    x1 = _layer_norm(DEEPNORM_ALPHA * x_ref[0] + gate_m * mix) * ln1g_ref[...] + ln1b_ref[...]
    h = (_layer_norm(x1) * (1.0 + scale_f) + shift_f).astype(BF16)

    @pl.when(si == 0)
    def _():
        carry_ref[...] = jnp.zeros(carry_ref.shape, F32)

    w = 2 * FF_CHUNK
    for c in range(N_FF_CHUNKS):
        sl = slice(c * w, (c + 1) * w)
        u = jnp.dot(h, wup_ref[:, sl], preferred_element_type=F32)
        ubuf_ref[0:8, :] = carry_ref[:, sl]
        ubuf_ref[8:t + 8, :] = u
        u1 = ubuf_ref[7:t + 7, :]
        u2 = ubuf_ref[6:t + 6, :]
        carry_ref[:, sl] = ubuf_ref[t:t + 8, :]
        y = fcb_ref[:, sl] + fcw_ref[0:1, sl] * u2 + fcw_ref[1:2, sl] * u1 + fcw_ref[2:3, sl] * u
        g = y[:, :FF_CHUNK]
        act_ref[:, c * FF_CHUNK:(c + 1) * FF_CHUNK] = (g * _sigmoid(g) * y[:, FF_CHUNK:]).astype(BF16)

    ff = jnp.dot(act_ref[...], wdn_ref[...], preferred_element_type=F32)
    o_ref[0] = _layer_norm(DEEPNORM_ALPHA * x1 + gate_f * ff) * ln2g_ref[...] + ln2b_ref[...]


def _out_ffn(x, ya, yc, mod, woa, woc, ln1g, ln1b, wup, fcw, fcb, wdn, ln2g, ln2b):
    b, s, d = x.shape
    t = TOK_TILE
    const = lambda shape: pl.BlockSpec(shape, lambda bi, si: (0,) * len(shape),
                                       pipeline_mode=pl.Buffered(1))
    tok = lambda width: pl.BlockSpec((1, t, width), lambda bi, si: (bi, si, 0))
    return pl.pallas_call(
        _ffn_kernel,
        grid=(b, s // t),
        in_specs=[tok(d), tok(ATTN_WIDTH), tok(CONV_WIDTH),
                  pl.BlockSpec((1, 6, d), lambda bi, si: (bi, 0, 0)),
                  const(woa.shape), const(woc.shape), const(ln1g.shape), const(ln1b.shape),
                  const(wup.shape), const(fcw.shape), const(fcb.shape), const(wdn.shape),
                  const(ln2g.shape), const(ln2b.shape)],
        out_specs=tok(d),
        out_shape=jax.ShapeDtypeStruct((b, s, d), F32),
        scratch_shapes=[pltpu.VMEM((t + 8, 2 * FF_CHUNK), F32),
                        pltpu.VMEM((8, 2 * D_FF), F32),
                        pltpu.VMEM((t, D_FF), BF16)],
        compiler_params=pltpu.CompilerParams(dimension_semantics=("arbitrary", "arbitrary"),
                                             vmem_limit_bytes=56 * 1024 * 1024),
        name="out_ffn",
    )(x, ya, yc, mod, woa, woc, ln1g, ln1b, wup, fcw, fcb, wdn, ln2g, ln2b)


def _swap_halves(w):
    half = w.shape[-1] // 2
    return jnp.concatenate([w[..., half:], w[..., :half]], axis=-1)


def _prep_in_proj(w_in):
    c0 = Q_LORA_RANK + KV_LORA_RANK
    kr = w_in[:, c0:c0 + QK_ROPE_DIM]
    krs = _swap_halves(kr)
    return jnp.concatenate([w_in[:, :c0], kr, krs, kr, krs, w_in[:, c0 + QK_ROPE_DIM:]], axis=1).astype(BF16)


def _prep_q_up(w_q_up):
    w = w_q_up.reshape(Q_LORA_RANK, MLA_HEADS, QK_NOPE_DIM + QK_ROPE_DIM)
    rope = w[..., QK_NOPE_DIM:]
    w = jnp.concatenate([w[..., :QK_NOPE_DIM], rope, _swap_halves(rope)], axis=-1)
    return w.reshape(Q_LORA_RANK, MLA_HEADS * HEAD_LANES).astype(BF16)


def _prep_kv_up(w_kv_up):
    w = w_kv_up.reshape(KV_LORA_RANK, MLA_HEADS, QK_NOPE_DIM + V_HEAD_DIM)
    knope = w[..., :QK_NOPE_DIM]
    wk = jnp.concatenate([knope, jnp.zeros_like(knope)], axis=-1).reshape(KV_LORA_RANK, MLA_HEADS * HEAD_LANES)
    wvt = w[..., QK_NOPE_DIM:].reshape(KV_LORA_RANK, ATTN_WIDTH).T
    return wk.astype(BF16), wvt.astype(BF16)


def _interleave_ff(a):
    lead = a.shape[:-1]
    g = a[..., :D_FF].reshape(lead + (N_FF_CHUNKS, FF_CHUNK))
    v = a[..., D_FF:].reshape(lead + (N_FF_CHUNKS, FF_CHUNK))
    return jnp.concatenate([g, v], axis=-1).reshape(lead + (2 * D_FF,))


def kernel(x, c, positions, w_ada, b_ada, w_in, q_norm_g, w_q_up, kv_norm_g, w_kv_up, conv_w, conv_b,
           out_norm_g, w_out, ln1_g, ln1_b, w_up, ffn_conv_w, ffn_conv_b, w_down, ln2_g, ln2_b):
    b, s, d = x.shape
    depth = w_ada.shape[0]
    tab = _rope_table(positions)
    for l in range(depth):
        mod = _adaln_mod(c, w_ada[l], b_ada[l]).reshape(b, 6, d)
        wk, wvt = _prep_kv_up(w_kv_up[l])
        q, k, vt, yc = _mixer_proj(
            x, mod, tab, _prep_in_proj(w_in[l]), q_norm_g[l].reshape(1, -1), _prep_q_up(w_q_up[l]),
            kv_norm_g[l].reshape(1, -1), wk, wvt, conv_w[l], conv_b[l].reshape(1, -1),
            out_norm_g[l, ATTN_WIDTH:].reshape(1, -1))
        ya = _attention(q, k, vt, out_norm_g[l, :ATTN_WIDTH].reshape(1, -1))
        x = _out_ffn(
            x, ya, yc, mod, w_out[l, :ATTN_WIDTH].astype(BF16), w_out[l, ATTN_WIDTH:].astype(BF16),
            ln1_g[l].reshape(1, -1), ln1_b[l].reshape(1, -1), _interleave_ff(w_up[l]).astype(BF16),
            _interleave_ff(ffn_conv_w[l]), _interleave_ff(ffn_conv_b[l]).reshape(1, -1),
            w_down[l].astype(BF16), ln2_g[l].reshape(1, -1), ln2_b[l].reshape(1, -1))
    return x
```

```python
import jax
import jax.numpy as jnp
from jax import lax
from jax.experimental import pallas as pl
from jax.experimental.pallas import tpu as pltpu

D_MODEL = 1024
SEQ = 2048
CHUNK = 64
MLA_HEADS = 8
QK_NOPE_DIM = 64
QK_ROPE_DIM = 32
V_HEAD_DIM = 64
Q_LORA_RANK = 256
KV_LORA_RANK = 128
ATTN_WIDTH = MLA_HEADS * V_HEAD_DIM
CONV_WIDTH = D_MODEL - ATTN_WIDTH
CONV_GROUPS = 8
D_FF = 2816
ROPE_THETA = 10000.0
RMS_EPS = 1e-6
LN_EPS = 1e-5
DEEPNORM_ALPHA = 2.0 ** 0.25

LANES = 128
HEAD_LANES = 128
IN_EXT = Q_LORA_RANK + KV_LORA_RANK + LANES + 3 * CONV_WIDTH
TOK_TILE = 512
Q_TILE = 256
K_TILE = 256
FF_CHUNK = 256
N_FF_CHUNKS = D_FF // FF_CHUNK
NEG = -1e30
Q_SCALE = (QK_NOPE_DIM + QK_ROPE_DIM) ** -0.5 * 1.4426950408889634

F32 = jnp.float32
BF16 = jnp.bfloat16


def _layer_norm(x):
    mu = jnp.mean(x, axis=-1, keepdims=True)
    xc = x - mu
    var = jnp.mean(xc * xc, axis=-1, keepdims=True)
    return xc * lax.rsqrt(var + LN_EPS)


def _rms(x):
    return x * lax.rsqrt(jnp.mean(x * x, axis=-1, keepdims=True) + RMS_EPS)


def _sigmoid(x):
    return 1.0 / (1.0 + jnp.exp(-x))


def _split3(x):
    hi = x.astype(BF16)
    r1 = x - hi.astype(F32)
    mid = r1.astype(BF16)
    lo = (r1 - mid.astype(F32)).astype(BF16)
    return hi, mid, lo


def _rope_kernel(pos_ref, freq_ref, e_ref, tab_ref):
    ang = pos_ref[...] * freq_ref[...]
    pieces = _split3(jnp.cos(ang)) + _split3(jnp.sin(ang))
    acc = None
    for i, piece in enumerate(pieces):
        part = jnp.dot(piece, e_ref[i // 3], preferred_element_type=F32)
        acc = part if acc is None else acc + part
    tab_ref[...] = acc


def _rope_expansion():
    half = QK_ROPE_DIM // 2
    src = jnp.arange(LANES)[:, None]
    dst = jnp.arange(8 * LANES)[None, :]
    same = (src // half == dst // LANES) & (src % half == dst % half)
    group = (dst % LANES) // half % 4
    e_cos = jnp.where(same & (group < 2), 1.0, 0.0)
    e_sin = jnp.where(same & (group == 2), -1.0, jnp.where(same & (group == 3), 1.0, 0.0))
    return jnp.stack([e_cos, e_sin]).astype(BF16)


def _rope_table(positions):
    b, s = positions.shape
    half = QK_ROPE_DIM // 2
    rows = b * s * half // LANES
    blk = 1024
    inv_freq = ROPE_THETA ** (-jnp.arange(0, QK_ROPE_DIM, 2, dtype=F32) / QK_ROPE_DIM)
    pos = jnp.repeat(positions.astype(F32), half, axis=-1).reshape(rows, LANES)
    freq = jnp.tile(inv_freq, LANES // half).reshape(1, LANES)
    tab = pl.pallas_call(
        _rope_kernel,
        grid=(rows // blk,),
        in_specs=[pl.BlockSpec((blk, LANES), lambda i: (i, 0)),
                  pl.BlockSpec((1, LANES), lambda i: (0, 0)),
                  pl.BlockSpec((2, LANES, 8 * LANES), lambda i: (0, 0, 0))],
        out_specs=pl.BlockSpec((blk, 8 * LANES), lambda i: (i, 0)),
        out_shape=jax.ShapeDtypeStruct((rows, 8 * LANES), F32),
        compiler_params=pltpu.CompilerParams(dimension_semantics=("arbitrary",),
                                             vmem_limit_bytes=32 * 1024 * 1024),
        name="rope_table",
    )(pos, freq, _rope_expansion())
    return tab.reshape(b, s, LANES)


def _mod_kernel(c_ref, w_ref, b_ref, o_ref):
    c = c_ref[...]
    act = c * _sigmoid(c)
    o_ref[...] = jnp.dot(act, w_ref[...], preferred_element_type=F32,
                         precision=lax.Precision.HIGHEST) + b_ref[...]


def _adaln_mod(c, w_ada, b_ada):
    b, d = c.shape
    n = w_ada.shape[1]
    return pl.pallas_call(
        _mod_kernel,
        grid=(n // d,),
        in_specs=[pl.BlockSpec((b, d), lambda j: (0, 0)),
                  pl.BlockSpec((d, d), lambda j: (0, j)),
                  pl.BlockSpec((1, d), lambda j: (0, j))],
        out_specs=pl.BlockSpec((b, d), lambda j: (0, j)),
        out_shape=jax.ShapeDtypeStruct((b, n), F32),
        compiler_params=pltpu.CompilerParams(dimension_semantics=("arbitrary",),
                                             vmem_limit_bytes=32 * 1024 * 1024),
        name="adaln_mod",
    )(c, w_ada, b_ada.reshape(1, n))


def _proj_kernel(x_ref, mod_ref, tab_ref, win_ref, qg_ref, wq_ref, kvg_ref, wk_ref, wvt_ref,
                 cw_ref, cb_ref, og_ref, q_ref, k_ref, vt_ref, yc_ref, ubuf_ref):
    t = TOK_TILE
    si = pl.program_id(1)
    shift = mod_ref[0, 0:1, :]
    scale = mod_ref[0, 1:2, :]
    h = (_layer_norm(x_ref[0]) * (1.0 + scale) + shift).astype(BF16)
    proj = jnp.dot(h, win_ref[...], preferred_element_type=F32)

    tab = tab_ref[0]
    lane = lax.broadcasted_iota(jnp.int32, (1, LANES), 1)
    is_nope = lane < QK_NOPE_DIM

    cq = (_rms(proj[:, 0:Q_LORA_RANK]) * qg_ref[...]).astype(BF16)
    q = jnp.dot(cq, wq_ref[...], preferred_element_type=F32)
    q_tab = jnp.where(is_nope, Q_SCALE, Q_SCALE * tab)
    for hh in range(MLA_HEADS):
        sl = slice(hh * HEAD_LANES, (hh + 1) * HEAD_LANES)
        q_ref[0, :, sl] = (q[:, sl] * q_tab).astype(BF16)

    c0 = Q_LORA_RANK
    ckv = (_rms(proj[:, c0:c0 + KV_LORA_RANK]) * kvg_ref[...]).astype(BF16)
    knope = jnp.dot(ckv, wk_ref[...], preferred_element_type=F32)
    c1 = c0 + KV_LORA_RANK
    r = proj[:, c1:c1 + LANES] * tab
    kf = r + pltpu.roll(r, QK_ROPE_DIM, axis=1)
    kf = jnp.where(is_nope, 0.0, kf)
    for hh in range(MLA_HEADS):
        sl = slice(hh * HEAD_LANES, (hh + 1) * HEAD_LANES)
        k_ref[0, :, sl] = (knope[:, sl] + kf).astype(BF16)
    vt_ref[0] = lax.dot_general(wvt_ref[...], ckv, (((1,), (1,)), ((), ())),
                                preferred_element_type=F32).astype(BF16)

    c2 = c1 + LANES
    gate_b = proj[:, c2:c2 + CONV_WIDTH]
    u = proj[:, c2 + CONV_WIDTH:c2 + 2 * CONV_WIDTH] * proj[:, c2 + 2 * CONV_WIDTH:c2 + 3 * CONV_WIDTH]

    @pl.when(si == 0)
    def _():
        ubuf_ref[0:8, :] = jnp.zeros((8, CONV_WIDTH), F32)

    ubuf_ref[8:t + 8, :] = u
    u1 = ubuf_ref[7:t + 7, :]
    u2 = ubuf_ref[6:t + 6, :]
    ubuf_ref[0:8, :] = ubuf_ref[t:t + 8, :]
    y = cb_ref[...] + cw_ref[0:1, :] * u2 + cw_ref[1:2, :] * u1 + cw_ref[2:3, :] * u
    y = gate_b * y
    lo = lane < (CONV_WIDTH // CONV_GROUPS)
    for cc in range(CONV_WIDTH // LANES):
        sl = slice(cc * LANES, (cc + 1) * LANES)
        yv = y[:, sl]
        sq = yv * yv
        ms_lo = jnp.sum(jnp.where(lo, sq, 0.0), axis=-1, keepdims=True)
        ms_hi = jnp.sum(jnp.where(lo, 0.0, sq), axis=-1, keepdims=True)
        ms = jnp.where(lo, ms_lo, ms_hi) * (1.0 / (CONV_WIDTH // CONV_GROUPS))
        yc_ref[0, :, sl] = (yv * lax.rsqrt(ms + RMS_EPS) * og_ref[:, sl]).astype(BF16)


def _mixer_proj(x, mod, tab, win, qg, wq, kvg, wk, wvt, cw, cb, og):
    b, s, d = x.shape
    t = TOK_TILE
    nt = s // t
    const = lambda shape: pl.BlockSpec(shape, lambda bi, si: (0,) * len(shape))
    return pl.pallas_call(
        _proj_kernel,
        grid=(b, nt),
        in_specs=[pl.BlockSpec((1, t, d), lambda bi, si: (bi, si, 0)),
                  pl.BlockSpec((1, 6, d), lambda bi, si: (bi, 0, 0)),
                  pl.BlockSpec((1, t, LANES), lambda bi, si: (bi, si, 0)),
                  const(win.shape), const(qg.shape), const(wq.shape), const(kvg.shape),
                  const(wk.shape), const(wvt.shape), const(cw.shape), const(cb.shape), const(og.shape)],
        out_specs=[pl.BlockSpec((1, t, MLA_HEADS * HEAD_LANES), lambda bi, si: (bi, si, 0)),
                   pl.BlockSpec((1, t, MLA_HEADS * HEAD_LANES), lambda bi, si: (bi, si, 0)),
                   pl.BlockSpec((1, ATTN_WIDTH, t), lambda bi, si: (bi, 0, si)),
                   pl.BlockSpec((1, t, CONV_WIDTH), lambda bi, si: (bi, si, 0))],
        out_shape=[jax.ShapeDtypeStruct((b, s, MLA_HEADS * HEAD_LANES), BF16),
                   jax.ShapeDtypeStruct((b, s, MLA_HEADS * HEAD_LANES), BF16),
                   jax.ShapeDtypeStruct((b, ATTN_WIDTH, s), BF16),
                   jax.ShapeDtypeStruct((b, s, CONV_WIDTH), BF16)],
        scratch_shapes=[pltpu.VMEM((t + 8, CONV_WIDTH), F32)],
        compiler_params=pltpu.CompilerParams(dimension_semantics=("arbitrary", "arbitrary"),
                                             vmem_limit_bytes=56 * 1024 * 1024),
        name="mixer_proj",
    )(x, mod, tab, win, qg, wq, kvg, wk, wvt, cw, cb, og)


def _attn_kernel(q_ref, k_ref, vt_ref, g_ref, o_ref, s_ref, p_ref, ot_ref):
    seq = q_ref.shape[1]
    key_chunk = lax.broadcasted_iota(jnp.int32, (K_TILE, Q_TILE), 0) // CHUNK
    qry_chunk = lax.broadcasted_iota(jnp.int32, (K_TILE, Q_TILE), 1) // CHUNK
    allowed = key_chunk <= qry_chunk
    dn = (((1,), (1,)), ((), ()))
    for qi in range(seq // Q_TILE):
        q0 = qi * Q_TILE
        kmax = q0 + Q_TILE
        for hh in range(2):
            hl = slice(hh * HEAD_LANES, (hh + 1) * HEAD_LANES)
            q = q_ref[0, q0:kmax, hl]
            if qi > 0:
                s_ref[hh, 0:q0, :] = lax.dot_general(k_ref[0, 0:q0, hl], q, dn, preferred_element_type=F32)
            sd = lax.dot_general(k_ref[0, q0:kmax, hl], q, dn, preferred_element_type=F32)
            s_ref[hh, q0:kmax, :] = jnp.where(allowed, sd, NEG)
            m = None
            for j in range(qi + 1):
                mj = jnp.max(s_ref[hh, j * K_TILE:(j + 1) * K_TILE, :], axis=0, keepdims=True)
                m = mj if m is None else jnp.maximum(m, mj)
            l = None
            for j in range(qi + 1):
                rows = slice(j * K_TILE, (j + 1) * K_TILE)
                p = jnp.exp2(s_ref[hh, rows, :] - m)
                lj = jnp.sum(p, axis=0, keepdims=True)
                l = lj if l is None else l + lj
                p_ref[hh, rows, :] = p.astype(BF16)
            o = jnp.dot(vt_ref[0, hh * V_HEAD_DIM:(hh + 1) * V_HEAD_DIM, 0:kmax], p_ref[hh, 0:kmax, :],
                        preferred_element_type=F32)
            o = o / l
            ms = jnp.mean(o * o, axis=0, keepdims=True)
            ot_ref[hh * V_HEAD_DIM:(hh + 1) * V_HEAD_DIM, :] = o * lax.rsqrt(ms + RMS_EPS)
        o_ref[0, q0:kmax, :] = (ot_ref[...].T * g_ref[...]).astype(BF16)


def _attention(q, k, vt, g):
    b, s, _ = q.shape
    pairs = MLA_HEADS // 2
    return pl.pallas_call(
        _attn_kernel,
        grid=(b, pairs),
        in_specs=[pl.BlockSpec((1, s, 2 * HEAD_LANES), lambda bi, pi: (bi, 0, pi)),
                  pl.BlockSpec((1, s, 2 * HEAD_LANES), lambda bi, pi: (bi, 0, pi)),
                  pl.BlockSpec((1, 2 * V_HEAD_DIM, s), lambda bi, pi: (bi, pi, 0)),
                  pl.BlockSpec((1, 2 * V_HEAD_DIM), lambda bi, pi: (0, pi))],
        out_specs=pl.BlockSpec((1, s, 2 * V_HEAD_DIM), lambda bi, pi: (bi, 0, pi)),
        out_shape=jax.ShapeDtypeStruct((b, s, ATTN_WIDTH), BF16),
        scratch_shapes=[pltpu.VMEM((2, s, Q_TILE), F32),
                        pltpu.VMEM((2, s, Q_TILE), BF16),
                        pltpu.VMEM((2 * V_HEAD_DIM, Q_TILE), F32)],
        compiler_params=pltpu.CompilerParams(dimension_semantics=("arbitrary",) * 2,
                                             vmem_limit_bytes=32 * 1024 * 1024),
        name="attention",
    )(q, k, vt, g)


def _ffn_kernel(x_ref, ya_ref, yc_ref, mod_ref, woa_ref, woc_ref, ln1g_ref, ln1b_ref, wup_ref,
                fcw_ref, fcb_ref, wdn_ref, ln2g_ref, ln2b_ref, o_ref, ubuf_ref, carry_ref, act_ref):
    t = TOK_TILE
    si = pl.program_id(1)
    gate_m = mod_ref[0, 2:3, :]
    shift_f = mod_ref[0, 3:4, :]
    scale_f = mod_ref[0, 4:5, :]
    gate_f = mod_ref[0, 5:6, :]

    mix = (jnp.dot(ya_ref[0], woa_ref[...], preferred_element_type=F32)
           + jnp.dot(yc_ref[0], woc_ref[...], preferred_element_type=F32))
    x1 = _layer_norm(DEEPNORM_ALPHA * x_ref[0] + gate_m * mix) * ln1g_ref[...] + ln1b_ref[...]
    h = (_layer_norm(x1) * (1.0 + scale_f) + shift_f).astype(BF16)

    @pl.when(si == 0)
    def _():
        carry_ref[...] = jnp.zeros(carry_ref.shape, F32)

    w = 2 * FF_CHUNK
    for c in range(N_FF_CHUNKS):
        sl = slice(c * w, (c + 1) * w)
        u = jnp.dot(h, wup_ref[:, sl], preferred_element_type=F32)
        ubuf_ref[0:8, :] = carry_ref[:, sl]
        ubuf_ref[8:t + 8, :] = u
        u1 = ubuf_ref[7:t + 7, :]
        u2 = ubuf_ref[6:t + 6, :]
        carry_ref[:, sl] = ubuf_ref[t:t + 8, :]
        y = fcb_ref[:, sl] + fcw_ref[0:1, sl] * u2 + fcw_ref[1:2, sl] * u1 + fcw_ref[2:3, sl] * u
        g = y[:, :FF_CHUNK]
        act_ref[:, c * FF_CHUNK:(c + 1) * FF_CHUNK] = (g * _sigmoid(g) * y[:, FF_CHUNK:]).astype(BF16)

    ff = jnp.dot(act_ref[...], wdn_ref[...], preferred_element_type=F32)
    o_ref[0] = _layer_norm(DEEPNORM_ALPHA * x1 + gate_f * ff) * ln2g_ref[...] + ln2b_ref[...]


def _out_ffn(x, ya, yc, mod, woa, woc, ln1g, ln1b, wup, fcw, fcb, wdn, ln2g, ln2b):
    b, s, d = x.shape
    t = TOK_TILE
    const = lambda shape: pl.BlockSpec(shape, lambda bi, si: (0,) * len(shape),
                                       pipeline_mode=pl.Buffered(1))
    tok = lambda width: pl.BlockSpec((1, t, width), lambda bi, si: (bi, si, 0))
    return pl.pallas_call(
        _ffn_kernel,
        grid=(b, s // t),
        in_specs=[tok(d), tok(ATTN_WIDTH), tok(CONV_WIDTH),
                  pl.BlockSpec((1, 6, d), lambda bi, si: (bi, 0, 0)),
                  const(woa.shape), const(woc.shape), const(ln1g.shape), const(ln1b.shape),
                  const(wup.shape), const(fcw.shape), const(fcb.shape), const(wdn.shape),
                  const(ln2g.shape), const(ln2b.shape)],
        out_specs=tok(d),
        out_shape=jax.ShapeDtypeStruct((b, s, d), F32),
        scratch_shapes=[pltpu.VMEM((t + 8, 2 * FF_CHUNK), F32),
                        pltpu.VMEM((8, 2 * D_FF), F32),
                        pltpu.VMEM((t, D_FF), BF16)],
        compiler_params=pltpu.CompilerParams(dimension_semantics=("arbitrary", "arbitrary"),
                                             vmem_limit_bytes=56 * 1024 * 1024),
        name="out_ffn",
    )(x, ya, yc, mod, woa, woc, ln1g, ln1b, wup, fcw, fcb, wdn, ln2g, ln2b)


def _swap_halves(w):
    half = w.shape[-1] // 2
    return jnp.concatenate([w[..., half:], w[..., :half]], axis=-1)


def _prep_in_proj(w_in):
    c0 = Q_LORA_RANK + KV_LORA_RANK
    kr = w_in[:, c0:c0 + QK_ROPE_DIM]
    krs = _swap_halves(kr)
    return jnp.concatenate([w_in[:, :c0], kr, krs, kr, krs, w_in[:, c0 + QK_ROPE_DIM:]], axis=1).astype(BF16)


def _prep_q_up(w_q_up):
    w = w_q_up.reshape(Q_LORA_RANK, MLA_HEADS, QK_NOPE_DIM + QK_ROPE_DIM)
    rope = w[..., QK_NOPE_DIM:]
    w = jnp.concatenate([w[..., :QK_NOPE_DIM], rope, _swap_halves(rope)], axis=-1)
    return w.reshape(Q_LORA_RANK, MLA_HEADS * HEAD_LANES).astype(BF16)


def _prep_kv_up(w_kv_up):
    w = w_kv_up.reshape(KV_LORA_RANK, MLA_HEADS, QK_NOPE_DIM + V_HEAD_DIM)
    knope = w[..., :QK_NOPE_DIM]
    wk = jnp.concatenate([knope, jnp.zeros_like(knope)], axis=-1).reshape(KV_LORA_RANK, MLA_HEADS * HEAD_LANES)
    wvt = w[..., QK_NOPE_DIM:].reshape(KV_LORA_RANK, ATTN_WIDTH).T
    return wk.astype(BF16), wvt.astype(BF16)


def _interleave_ff(a):
    lead = a.shape[:-1]
    g = a[..., :D_FF].reshape(lead + (N_FF_CHUNKS, FF_CHUNK))
    v = a[..., D_FF:].reshape(lead + (N_FF_CHUNKS, FF_CHUNK))
    return jnp.concatenate([g, v], axis=-1).reshape(lead + (2 * D_FF,))


def kernel(x, c, positions, w_ada, b_ada, w_in, q_norm_g, w_q_up, kv_norm_g, w_kv_up, conv_w, conv_b,
           out_norm_g, w_out, ln1_g, ln1_b, w_up, ffn_conv_w, ffn_conv_b, w_down, ln2_g, ln2_b):
    b, s, d = x.shape
    depth = w_ada.shape[0]
    tab = _rope_table(positions)
    for l in range(depth):
        mod = _adaln_mod(c, w_ada[l], b_ada[l]).reshape(b, 6, d)
        wk, wvt = _prep_kv_up(w_kv_up[l])
        q, k, vt, yc = _mixer_proj(
            x, mod, tab, _prep_in_proj(w_in[l]), q_norm_g[l].reshape(1, -1), _prep_q_up(w_q_up[l]),
            kv_norm_g[l].reshape(1, -1), wk, wvt, conv_w[l], conv_b[l].reshape(1, -1),
            out_norm_g[l, ATTN_WIDTH:].reshape(1, -1))
        ya = _attention(q, k, vt, out_norm_g[l, :ATTN_WIDTH].reshape(1, -1))
        x = _out_ffn(
            x, ya, yc, mod, w_out[l, :ATTN_WIDTH].astype(BF16), w_out[l, ATTN_WIDTH:].astype(BF16),
            ln1_g[l].reshape(1, -1), ln1_b[l].reshape(1, -1), _interleave_ff(w_up[l]).astype(BF16),
            _interleave_ff(ffn_conv_w[l]), _interleave_ff(ffn_conv_b[l]).reshape(1, -1),
            w_down[l].astype(BF16), ln2_g[l].reshape(1, -1), ln2_b[l].reshape(1, -1))
    return x
```

```python
import jax
import jax.numpy as jnp
from jax import lax
from jax.experimental import pallas as pl
from jax.experimental.pallas import tpu as pltpu

D_MODEL = 1024
SEQ = 2048
CHUNK = 64
MLA_HEADS = 8
QK_NOPE_DIM = 64
QK_ROPE_DIM = 32
V_HEAD_DIM = 64
Q_LORA_RANK = 256
KV_LORA_RANK = 128
ATTN_WIDTH = MLA_HEADS * V_HEAD_DIM
CONV_WIDTH = D_MODEL - ATTN_WIDTH
CONV_GROUPS = 8
D_FF = 2816
ROPE_THETA = 10000.0
RMS_EPS = 1e-6
LN_EPS = 1e-5
DEEPNORM_ALPHA = 2.0 ** 0.25

LANES = 128
HEAD_LANES = 128
IN_EXT = Q_LORA_RANK + KV_LORA_RANK + LANES + 3 * CONV_WIDTH
TOK_TILE = 512
Q_TILE = 512
K_TILE = 256
FF_CHUNK = 256
N_FF_CHUNKS = D_FF // FF_CHUNK
NEG = -1e30
Q_SCALE = (QK_NOPE_DIM + QK_ROPE_DIM) ** -0.5 * 1.4426950408889634

F32 = jnp.float32
BF16 = jnp.bfloat16


def _layer_norm(x):
    mu = jnp.mean(x, axis=-1, keepdims=True)
    xc = x - mu
    var = jnp.mean(xc * xc, axis=-1, keepdims=True)
    return xc * lax.rsqrt(var + LN_EPS)


def _rms(x):
    return x * lax.rsqrt(jnp.mean(x * x, axis=-1, keepdims=True) + RMS_EPS)


def _sigmoid(x):
    return 1.0 / (1.0 + jnp.exp(-x))


def _split3(x):
    hi = x.astype(BF16)
    r1 = x - hi.astype(F32)
    mid = r1.astype(BF16)
    lo = (r1 - mid.astype(F32)).astype(BF16)
    return hi, mid, lo


def _rope_kernel(pos_ref, freq_ref, e_ref, tab_ref):
    ang = pos_ref[...] * freq_ref[...]
    pieces = _split3(jnp.cos(ang)) + _split3(jnp.sin(ang))
    acc = None
    for i, piece in enumerate(pieces):
        part = jnp.dot(piece, e_ref[i // 3], preferred_element_type=F32)
        acc = part if acc is None else acc + part
    tab_ref[...] = acc


def _rope_expansion():
    half = QK_ROPE_DIM // 2
    src = jnp.arange(LANES)[:, None]
    dst = jnp.arange(8 * LANES)[None, :]
    same = (src // half == dst // LANES) & (src % half == dst % half)
    group = (dst % LANES) // half % 4
    e_cos = jnp.where(same & (group < 2), 1.0, 0.0)
    e_sin = jnp.where(same & (group == 2), -1.0, jnp.where(same & (group == 3), 1.0, 0.0))
    return jnp.stack([e_cos, e_sin]).astype(BF16)


def _rope_table(positions):
    b, s = positions.shape
    half = QK_ROPE_DIM // 2
    rows = b * s * half // LANES
    blk = 1024
    inv_freq = ROPE_THETA ** (-jnp.arange(0, QK_ROPE_DIM, 2, dtype=F32) / QK_ROPE_DIM)
    pos = jnp.repeat(positions.astype(F32), half, axis=-1).reshape(rows, LANES)
    freq = jnp.tile(inv_freq, LANES // half).reshape(1, LANES)
    tab = pl.pallas_call(
        _rope_kernel,
        grid=(rows // blk,),
        in_specs=[pl.BlockSpec((blk, LANES), lambda i: (i, 0)),
                  pl.BlockSpec((1, LANES), lambda i: (0, 0)),
                  pl.BlockSpec((2, LANES, 8 * LANES), lambda i: (0, 0, 0))],
        out_specs=pl.BlockSpec((blk, 8 * LANES), lambda i: (i, 0)),
        out_shape=jax.ShapeDtypeStruct((rows, 8 * LANES), F32),
        compiler_params=pltpu.CompilerParams(dimension_semantics=("arbitrary",),
                                             vmem_limit_bytes=32 * 1024 * 1024),
        name="rope_table",
    )(pos, freq, _rope_expansion())
    return tab.reshape(b, s, LANES)


def _mod_kernel(c_ref, w_ref, b_ref, o_ref):
    c = c_ref[...]
    act = c * _sigmoid(c)
    o_ref[...] = jnp.dot(act, w_ref[...], preferred_element_type=F32,
                         precision=lax.Precision.HIGHEST) + b_ref[...]


def _adaln_mod(c, w_ada, b_ada):
    b, d = c.shape
    n = w_ada.shape[1]
    return pl.pallas_call(
        _mod_kernel,
        grid=(n // d,),
        in_specs=[pl.BlockSpec((b, d), lambda j: (0, 0)),
                  pl.BlockSpec((d, d), lambda j: (0, j)),
                  pl.BlockSpec((1, d), lambda j: (0, j))],
        out_specs=pl.BlockSpec((b, d), lambda j: (0, j)),
        out_shape=jax.ShapeDtypeStruct((b, n), F32),
        compiler_params=pltpu.CompilerParams(dimension_semantics=("arbitrary",),
                                             vmem_limit_bytes=32 * 1024 * 1024),
        name="adaln_mod",
    )(c, w_ada, b_ada.reshape(1, n))


def _proj_kernel(x_ref, mod_ref, tab_ref, win_ref, qg_ref, wq_ref, kvg_ref, wk_ref, wvt_ref,
                 cw_ref, cb_ref, og_ref, q_ref, k_ref, vt_ref, yc_ref, ubuf_ref):
    t = TOK_TILE
    si = pl.program_id(1)
    shift = mod_ref[0, 0:1, :]
    scale = mod_ref[0, 1:2, :]
    h = (_layer_norm(x_ref[0]) * (1.0 + scale) + shift).astype(BF16)
    proj = jnp.dot(h, win_ref[...], preferred_element_type=F32)

    tab = tab_ref[0]
    lane = lax.broadcasted_iota(jnp.int32, (1, LANES), 1)
    is_nope = lane < QK_NOPE_DIM

    cq = (_rms(proj[:, 0:Q_LORA_RANK]) * qg_ref[...]).astype(BF16)
    q = jnp.dot(cq, wq_ref[...], preferred_element_type=F32)
    q_tab = jnp.where(is_nope, Q_SCALE, Q_SCALE * tab)
    for hh in range(MLA_HEADS):
        sl = slice(hh * HEAD_LANES, (hh + 1) * HEAD_LANES)
        q_ref[0, :, sl] = (q[:, sl] * q_tab).astype(BF16)

    c0 = Q_LORA_RANK
    ckv = (_rms(proj[:, c0:c0 + KV_LORA_RANK]) * kvg_ref[...]).astype(BF16)
    knope = jnp.dot(ckv, wk_ref[...], preferred_element_type=F32)
    c1 = c0 + KV_LORA_RANK
    r = proj[:, c1:c1 + LANES] * tab
    kf = r + pltpu.roll(r, QK_ROPE_DIM, axis=1)
    kf = jnp.where(is_nope, 0.0, kf)
    for hh in range(MLA_HEADS):
        sl = slice(hh * HEAD_LANES, (hh + 1) * HEAD_LANES)
        k_ref[0, :, sl] = (knope[:, sl] + kf).astype(BF16)
    vt_ref[0] = lax.dot_general(wvt_ref[...], ckv, (((1,), (1,)), ((), ())),
                                preferred_element_type=F32).astype(BF16)

    c2 = c1 + LANES
    gate_b = proj[:, c2:c2 + CONV_WIDTH]
    u = proj[:, c2 + CONV_WIDTH:c2 + 2 * CONV_WIDTH] * proj[:, c2 + 2 * CONV_WIDTH:c2 + 3 * CONV_WIDTH]

    @pl.when(si == 0)
    def _():
        ubuf_ref[0:8, :] = jnp.zeros((8, CONV_WIDTH), F32)

    ubuf_ref[8:t + 8, :] = u
    u1 = ubuf_ref[7:t + 7, :]
    u2 = ubuf_ref[6:t + 6, :]
    ubuf_ref[0:8, :] = ubuf_ref[t:t + 8, :]
    y = cb_ref[...] + cw_ref[0:1, :] * u2 + cw_ref[1:2, :] * u1 + cw_ref[2:3, :] * u
    y = gate_b * y
    lo = lane < (CONV_WIDTH // CONV_GROUPS)
    for cc in range(CONV_WIDTH // LANES):
        sl = slice(cc * LANES, (cc + 1) * LANES)
        yv = y[:, sl]
        sq = yv * yv
        ms_lo = jnp.sum(jnp.where(lo, sq, 0.0), axis=-1, keepdims=True)
        ms_hi = jnp.sum(jnp.where(lo, 0.0, sq), axis=-1, keepdims=True)
        ms = jnp.where(lo, ms_lo, ms_hi) * (1.0 / (CONV_WIDTH // CONV_GROUPS))
        yc_ref[0, :, sl] = (yv * lax.rsqrt(ms + RMS_EPS) * og_ref[:, sl]).astype(BF16)


def _mixer_proj(x, mod, tab, win, qg, wq, kvg, wk, wvt, cw, cb, og):
    b, s, d = x.shape
    t = TOK_TILE
    nt = s // t
    const = lambda shape: pl.BlockSpec(shape, lambda bi, si: (0,) * len(shape))
    return pl.pallas_call(
        _proj_kernel,
        grid=(b, nt),
        in_specs=[pl.BlockSpec((1, t, d), lambda bi, si: (bi, si, 0)),
                  pl.BlockSpec((1, 6, d), lambda bi, si: (bi, 0, 0)),
                  pl.BlockSpec((1, t, LANES), lambda bi, si: (bi, si, 0)),
                  const(win.shape), const(qg.shape), const(wq.shape), const(kvg.shape),
                  const(wk.shape), const(wvt.shape), const(cw.shape), const(cb.shape), const(og.shape)],
        out_specs=[pl.BlockSpec((1, t, MLA_HEADS * HEAD_LANES), lambda bi, si: (bi, si, 0)),
                   pl.BlockSpec((1, t, MLA_HEADS * HEAD_LANES), lambda bi, si: (bi, si, 0)),
                   pl.BlockSpec((1, ATTN_WIDTH, t), lambda bi, si: (bi, 0, si)),
                   pl.BlockSpec((1, t, CONV_WIDTH), lambda bi, si: (bi, si, 0))],
        out_shape=[jax.ShapeDtypeStruct((b, s, MLA_HEADS * HEAD_LANES), BF16),
                   jax.ShapeDtypeStruct((b, s, MLA_HEADS * HEAD_LANES), BF16),
                   jax.ShapeDtypeStruct((b, ATTN_WIDTH, s), BF16),
                   jax.ShapeDtypeStruct((b, s, CONV_WIDTH), BF16)],
        scratch_shapes=[pltpu.VMEM((t + 8, CONV_WIDTH), F32)],
        compiler_params=pltpu.CompilerParams(dimension_semantics=("arbitrary", "arbitrary"),
                                             vmem_limit_bytes=56 * 1024 * 1024),
        name="mixer_proj",
    )(x, mod, tab, win, qg, wq, kvg, wk, wvt, cw, cb, og)


def _attn_kernel(q_ref, k_ref, vt_ref, g_ref, o_ref, s_ref, p_ref, ot_ref):
    seq = q_ref.shape[1]
    key_chunk = lax.broadcasted_iota(jnp.int32, (Q_TILE, Q_TILE), 0) // CHUNK
    qry_chunk = lax.broadcasted_iota(jnp.int32, (Q_TILE, Q_TILE), 1) // CHUNK
    allowed = key_chunk <= qry_chunk
    dn = (((1,), (1,)), ((), ()))
    for qi in range(seq // Q_TILE):
        q0 = qi * Q_TILE
        kmax = q0 + Q_TILE
        for hh in range(2):
            hl = slice(hh * HEAD_LANES, (hh + 1) * HEAD_LANES)
            q = q_ref[0, q0:kmax, hl]
            if qi > 0:
                s_ref[hh, 0:q0, :] = lax.dot_general(k_ref[0, 0:q0, hl], q, dn, preferred_element_type=F32)
            sd = lax.dot_general(k_ref[0, q0:kmax, hl], q, dn, preferred_element_type=F32)
            s_ref[hh, q0:kmax, :] = jnp.where(allowed, sd, NEG)
            m = None
            for j in range(kmax // K_TILE):
                mj = jnp.max(s_ref[hh, j * K_TILE:(j + 1) * K_TILE, :], axis=0, keepdims=True)
                m = mj if m is None else jnp.maximum(m, mj)
            l = None
            for j in range(kmax // K_TILE):
                rows = slice(j * K_TILE, (j + 1) * K_TILE)
                p = jnp.exp2(s_ref[hh, rows, :] - m)
                lj = jnp.sum(p, axis=0, keepdims=True)
                l = lj if l is None else l + lj
                p_ref[hh, rows, :] = p.astype(BF16)
            o = jnp.dot(vt_ref[0, hh * V_HEAD_DIM:(hh + 1) * V_HEAD_DIM, 0:kmax], p_ref[hh, 0:kmax, :],
                        preferred_element_type=F32)
            o = o / l
            ms = jnp.mean(o * o, axis=0, keepdims=True)
            ot_ref[hh * V_HEAD_DIM:(hh + 1) * V_HEAD_DIM, :] = o * lax.rsqrt(ms + RMS_EPS)
        o_ref[0, q0:kmax, :] = (ot_ref[...].T * g_ref[...]).astype(BF16)


def _attention(q, k, vt, g):
    b, s, _ = q.shape
    pairs = MLA_HEADS // 2
    return pl.pallas_call(
        _attn_kernel,
        grid=(b, pairs),
        in_specs=[pl.BlockSpec((1, s, 2 * HEAD_LANES), lambda bi, pi: (bi, 0, pi)),
                  pl.BlockSpec((1, s, 2 * HEAD_LANES), lambda bi, pi: (bi, 0, pi)),
                  pl.BlockSpec((1, 2 * V_HEAD_DIM, s), lambda bi, pi: (bi, pi, 0)),
                  pl.BlockSpec((1, 2 * V_HEAD_DIM), lambda bi, pi: (0, pi))],
        out_specs=pl.BlockSpec((1, s, 2 * V_HEAD_DIM), lambda bi, pi: (bi, 0, pi)),
        out_shape=jax.ShapeDtypeStruct((b, s, ATTN_WIDTH), BF16),
        scratch_shapes=[pltpu.VMEM((2, s, Q_TILE), F32),
                        pltpu.VMEM((2, s, Q_TILE), BF16),
                        pltpu.VMEM((2 * V_HEAD_DIM, Q_TILE), F32)],
        compiler_params=pltpu.CompilerParams(dimension_semantics=("arbitrary",) * 2,
                                             vmem_limit_bytes=48 * 1024 * 1024),
        name="attention",
    )(q, k, vt, g)


def _ffn_kernel(x_ref, ya_ref, yc_ref, mod_ref, woa_ref, woc_ref, ln1g_ref, ln1b_ref, wup_ref,
                fcw_ref, fcb_ref, wdn_ref, ln2g_ref, ln2b_ref, o_ref, ubuf_ref, carry_ref, act_ref):
    t = TOK_TILE
    si = pl.program_id(1)
    gate_m = mod_ref[0, 2:3, :]
    shift_f = mod_ref[0, 3:4, :]
    scale_f = mod_ref[0, 4:5, :]
    gate_f = mod_ref[0, 5:6, :]

    mix = (jnp.dot(ya_ref[0], woa_ref[...], preferred_element_type=F32)
           + jnp.dot(yc_ref[0], woc_ref[...], preferred_element_type=F32))
    x1 = _layer_norm(DEEPNORM_ALPHA * x_ref[0] + gate_m * mix) * ln1g_ref[...] + ln1b_ref[...]
    h = (_layer_norm(x1) * (1.0 + scale_f) + shift_f).astype(BF16)

    @pl.when(si == 0)
    def _():
        carry_ref[...] = jnp.zeros(carry_ref.shape, F32)

    def conv_cols(slot, cols):
        u = jnp.dot(h, wup_ref[:, cols], preferred_element_type=F32)
        ubuf_ref[slot, 0:8, :] = carry_ref[:, cols]
        ubuf_ref[slot, 8:t + 8, :] = u
        u1 = ubuf_ref[slot, 7:t + 7, :]
        u2 = ubuf_ref[slot, 6:t + 6, :]
        carry_ref[:, cols] = ubuf_ref[slot, t:t + 8, :]
        return fcb_ref[:, cols] + fcw_ref[0:1, cols] * u2 + fcw_ref[1:2, cols] * u1 + fcw_ref[2:3, cols] * u

    for c in range(N_FF_CHUNKS):
        g = conv_cols(0, slice(c * FF_CHUNK, (c + 1) * FF_CHUNK))
        v = conv_cols(1, slice(D_FF + c * FF_CHUNK, D_FF + (c + 1) * FF_CHUNK))
        act_ref[:, c * FF_CHUNK:(c + 1) * FF_CHUNK] = (g * _sigmoid(g) * v).astype(BF16)

    ff = jnp.dot(act_ref[...], wdn_ref[...], preferred_element_type=F32)
    o_ref[0] = _layer_norm(DEEPNORM_ALPHA * x1 + gate_f * ff) * ln2g_ref[...] + ln2b_ref[...]


def _out_ffn(x, ya, yc, mod, woa, woc, ln1g, ln1b, wup, fcw, fcb, wdn, ln2g, ln2b):
    b, s, d = x.shape
    t = TOK_TILE
    const = lambda shape: pl.BlockSpec(shape, lambda bi, si: (0,) * len(shape),
                                       pipeline_mode=pl.Buffered(1))
    tok = lambda width: pl.BlockSpec((1, t, width), lambda bi, si: (bi, si, 0))
    return pl.pallas_call(
        _ffn_kernel,
        grid=(b, s // t),
        in_specs=[tok(d), tok(ATTN_WIDTH), tok(CONV_WIDTH),
                  pl.BlockSpec((1, 6, d), lambda bi, si: (bi, 0, 0)),
                  const(woa.shape), const(woc.shape), const(ln1g.shape), const(ln1b.shape),
                  const(wup.shape), const(fcw.shape), const(fcb.shape), const(wdn.shape),
                  const(ln2g.shape), const(ln2b.shape)],
        out_specs=tok(d),
        out_shape=jax.ShapeDtypeStruct((b, s, d), F32),
        scratch_shapes=[pltpu.VMEM((2, t + 8, FF_CHUNK), F32),
                        pltpu.VMEM((8, 2 * D_FF), F32),
                        pltpu.VMEM((t, D_FF), BF16)],
        compiler_params=pltpu.CompilerParams(dimension_semantics=("arbitrary", "arbitrary"),
                                             vmem_limit_bytes=56 * 1024 * 1024),
        name="out_ffn",
    )(x, ya, yc, mod, woa, woc, ln1g, ln1b, wup, fcw, fcb, wdn, ln2g, ln2b)


def _swap_halves(w):
    half = w.shape[-1] // 2
    return jnp.concatenate([w[..., half:], w[..., :half]], axis=-1)


def _prep_in_proj(w_in):
    c0 = Q_LORA_RANK + KV_LORA_RANK
    kr = w_in[:, c0:c0 + QK_ROPE_DIM]
    krs = _swap_halves(kr)
    return jnp.concatenate([w_in[:, :c0], kr, krs, kr, krs, w_in[:, c0 + QK_ROPE_DIM:]], axis=1).astype(BF16)


def _prep_q_up(w_q_up):
    w = w_q_up.reshape(Q_LORA_RANK, MLA_HEADS, QK_NOPE_DIM + QK_ROPE_DIM)
    rope = w[..., QK_NOPE_DIM:]
    w = jnp.concatenate([w[..., :QK_NOPE_DIM], rope, _swap_halves(rope)], axis=-1)
    return w.reshape(Q_LORA_RANK, MLA_HEADS * HEAD_LANES).astype(BF16)


def _prep_kv_up(w_kv_up):
    w = w_kv_up.reshape(KV_LORA_RANK, MLA_HEADS, QK_NOPE_DIM + V_HEAD_DIM)
    knope = w[..., :QK_NOPE_DIM]
    wk = jnp.concatenate([knope, jnp.zeros_like(knope)], axis=-1).reshape(KV_LORA_RANK, MLA_HEADS * HEAD_LANES)
    wvt = w[..., QK_NOPE_DIM:].reshape(KV_LORA_RANK, ATTN_WIDTH).T
    return wk.astype(BF16), wvt.astype(BF16)


def kernel(x, c, positions, w_ada, b_ada, w_in, q_norm_g, w_q_up, kv_norm_g, w_kv_up, conv_w, conv_b,
           out_norm_g, w_out, ln1_g, ln1_b, w_up, ffn_conv_w, ffn_conv_b, w_down, ln2_g, ln2_b):
    b, s, d = x.shape
    depth = w_ada.shape[0]
    tab = _rope_table(positions)
    for l in range(depth):
        mod = _adaln_mod(c, w_ada[l], b_ada[l]).reshape(b, 6, d)
        wk, wvt = _prep_kv_up(w_kv_up[l])
        q, k, vt, yc = _mixer_proj(
            x, mod, tab, _prep_in_proj(w_in[l]), q_norm_g[l].reshape(1, -1), _prep_q_up(w_q_up[l]),
            kv_norm_g[l].reshape(1, -1), wk, wvt, conv_w[l], conv_b[l].reshape(1, -1),
            out_norm_g[l, ATTN_WIDTH:].reshape(1, -1))
        ya = _attention(q, k, vt, out_norm_g[l, :ATTN_WIDTH].reshape(1, -1))
        x = _out_ffn(
            x, ya, yc, mod, w_out[l, :ATTN_WIDTH].astype(BF16), w_out[l, ATTN_WIDTH:].astype(BF16),
            ln1_g[l].reshape(1, -1), ln1_b[l].reshape(1, -1), w_up[l].astype(BF16),
            ffn_conv_w[l], ffn_conv_b[l].reshape(1, -1),
            w_down[l].astype(BF16), ln2_g[l].reshape(1, -1), ln2_b[l].reshape(1, -1))
    return x
```

```python
import jax
import jax.numpy as jnp
from jax import lax
from jax.experimental import pallas as pl
from jax.experimental.pallas import tpu as pltpu

D_MODEL = 1024
SEQ = 2048
CHUNK = 64
MLA_HEADS = 8
QK_NOPE_DIM = 64
QK_ROPE_DIM = 32
V_HEAD_DIM = 64
Q_LORA_RANK = 256
KV_LORA_RANK = 128
ATTN_WIDTH = MLA_HEADS * V_HEAD_DIM
CONV_WIDTH = D_MODEL - ATTN_WIDTH
CONV_GROUPS = 8
D_FF = 2816
ROPE_THETA = 10000.0
RMS_EPS = 1e-6
LN_EPS = 1e-5
DEEPNORM_ALPHA = 2.0 ** 0.25

LANES = 128
HEAD_LANES = 128
IN_EXT = Q_LORA_RANK + KV_LORA_RANK + LANES + 3 * CONV_WIDTH
TOK_TILE = 512
Q_TILE = 512
K_TILE = 256
FF_CHUNK = 256
N_FF_CHUNKS = D_FF // FF_CHUNK
FFN_ROW_BLOCKS = 2
PROJ_ROW_BLOCKS = 2
NEG = -1e30
Q_SCALE = (QK_NOPE_DIM + QK_ROPE_DIM) ** -0.5 * 1.4426950408889634

F32 = jnp.float32
BF16 = jnp.bfloat16
SCHED_FLAGS = None


def _layer_norm(x):
    mu = jnp.mean(x, axis=-1, keepdims=True)
    xc = x - mu
    var = jnp.mean(xc * xc, axis=-1, keepdims=True)
    return xc * lax.rsqrt(var + LN_EPS)


def _rms(x):
    return x * lax.rsqrt(jnp.mean(x * x, axis=-1, keepdims=True) + RMS_EPS)


def _sigmoid(x):
    return 1.0 / (1.0 + jnp.exp(-x))


def _split3(x):
    hi = x.astype(BF16)
    r1 = x - hi.astype(F32)
    mid = r1.astype(BF16)
    lo = (r1 - mid.astype(F32)).astype(BF16)
    return hi, mid, lo


def _rope_kernel(pos_ref, freq_ref, e_ref, tab_ref):
    ang = pos_ref[...] * freq_ref[...]
    pieces = _split3(jnp.cos(ang)) + _split3(jnp.sin(ang))
    acc = None
    for i, piece in enumerate(pieces):
        part = jnp.dot(piece, e_ref[i // 3], preferred_element_type=F32)
        acc = part if acc is None else acc + part
    tab_ref[...] = acc


def _rope_expansion():
    half = QK_ROPE_DIM // 2
    src = jnp.arange(LANES)[:, None]
    dst = jnp.arange(8 * LANES)[None, :]
    same = (src // half == dst // LANES) & (src % half == dst % half)
    group = (dst % LANES) // half % 4
    e_cos = jnp.where(same & (group < 2), 1.0, 0.0)
    e_sin = jnp.where(same & (group == 2), -1.0, jnp.where(same & (group == 3), 1.0, 0.0))
    return jnp.stack([e_cos, e_sin]).astype(BF16)


def _rope_table(positions):
    b, s = positions.shape
    half = QK_ROPE_DIM // 2
    rows = b * s * half // LANES
    blk = 1024
    inv_freq = ROPE_THETA ** (-jnp.arange(0, QK_ROPE_DIM, 2, dtype=F32) / QK_ROPE_DIM)
    pos = jnp.repeat(positions.astype(F32), half, axis=-1).reshape(rows, LANES)
    freq = jnp.tile(inv_freq, LANES // half).reshape(1, LANES)
    tab = pl.pallas_call(
        _rope_kernel,
        grid=(rows // blk,),
        in_specs=[pl.BlockSpec((blk, LANES), lambda i: (i, 0)),
                  pl.BlockSpec((1, LANES), lambda i: (0, 0)),
                  pl.BlockSpec((2, LANES, 8 * LANES), lambda i: (0, 0, 0))],
        out_specs=pl.BlockSpec((blk, 8 * LANES), lambda i: (i, 0)),
        out_shape=jax.ShapeDtypeStruct((rows, 8 * LANES), F32),
        compiler_params=pltpu.CompilerParams(dimension_semantics=("arbitrary",),
                                             vmem_limit_bytes=32 * 1024 * 1024),
        name="rope_table",
    )(pos, freq, _rope_expansion())
    return tab.reshape(b, s, LANES)


def _mod_kernel(c_ref, w_ref, b_ref, o_ref):
    c = c_ref[...]
    act = c * _sigmoid(c)
    o_ref[...] = jnp.dot(act, w_ref[...], preferred_element_type=F32,
                         precision=lax.Precision.HIGHEST) + b_ref[...]


def _adaln_mod(c, w_ada, b_ada):
    b, d = c.shape
    n = w_ada.shape[1]
    return pl.pallas_call(
        _mod_kernel,
        grid=(n // d,),
        in_specs=[pl.BlockSpec((b, d), lambda j: (0, 0)),
                  pl.BlockSpec((d, d), lambda j: (0, j)),
                  pl.BlockSpec((1, d), lambda j: (0, j))],
        out_specs=pl.BlockSpec((b, d), lambda j: (0, j)),
        out_shape=jax.ShapeDtypeStruct((b, n), F32),
        compiler_params=pltpu.CompilerParams(dimension_semantics=("arbitrary",),
                                             vmem_limit_bytes=32 * 1024 * 1024),
        name="adaln_mod",
    )(c, w_ada, b_ada.reshape(1, n))


def _proj_kernel(x_ref, mod_ref, tab_ref, win_ref, qg_ref, wq_ref, kvg_ref, wk_ref, wvt_ref,
                 cw_ref, cb_ref, og_ref, q_ref, k_ref, vt_ref, yc_ref, ubuf_ref):
    t = TOK_TILE
    si = pl.program_id(1)
    shift = mod_ref[0, 0:1, :]
    scale = mod_ref[0, 1:2, :]
    lane = lax.broadcasted_iota(jnp.int32, (1, LANES), 1)
    is_nope = lane < QK_NOPE_DIM
    lo = lane < (CONV_WIDTH // CONV_GROUPS)
    c0 = Q_LORA_RANK
    c1 = c0 + KV_LORA_RANK
    c2 = c1 + LANES

    @pl.when(si == 0)
    def _():
        ubuf_ref[0:8, :] = jnp.zeros((8, CONV_WIDTH), F32)

    for r0 in range(0, t, t // PROJ_ROW_BLOCKS):
        r1 = r0 + t // PROJ_ROW_BLOCKS
        h = (_layer_norm(x_ref[0, r0:r1, :]) * (1.0 + scale) + shift).astype(BF16)
        proj = jnp.dot(h, win_ref[...], preferred_element_type=F32)
        tab = tab_ref[0, r0:r1, :]

        cq = (_rms(proj[:, 0:Q_LORA_RANK]) * qg_ref[...]).astype(BF16)
        q = jnp.dot(cq, wq_ref[...], preferred_element_type=F32)
        q_tab = jnp.where(is_nope, Q_SCALE, Q_SCALE * tab)
        for hh in range(MLA_HEADS):
            sl = slice(hh * HEAD_LANES, (hh + 1) * HEAD_LANES)
            q_ref[0, r0:r1, sl] = (q[:, sl] * q_tab).astype(BF16)

        ckv = (_rms(proj[:, c0:c0 + KV_LORA_RANK]) * kvg_ref[...]).astype(BF16)
        knope = jnp.dot(ckv, wk_ref[...], preferred_element_type=F32)
        r = proj[:, c1:c1 + LANES] * tab
        kf = r + pltpu.roll(r, QK_ROPE_DIM, axis=1)
        kf = jnp.where(is_nope, 0.0, kf)
        for hh in range(MLA_HEADS):
            sl = slice(hh * HEAD_LANES, (hh + 1) * HEAD_LANES)
            k_ref[0, r0:r1, sl] = (knope[:, sl] + kf).astype(BF16)
        vt_ref[0, :, r0:r1] = lax.dot_general(wvt_ref[...], ckv, (((1,), (1,)), ((), ())),
                                              preferred_element_type=F32).astype(BF16)

        gate_b = proj[:, c2:c2 + CONV_WIDTH]
        u = proj[:, c2 + CONV_WIDTH:c2 + 2 * CONV_WIDTH] * proj[:, c2 + 2 * CONV_WIDTH:c2 + 3 * CONV_WIDTH]
        ubuf_ref[r0 + 8:r1 + 8, :] = u
        u1 = ubuf_ref[r0 + 7:r1 + 7, :]
        u2 = ubuf_ref[r0 + 6:r1 + 6, :]
        if r1 == t:
            ubuf_ref[0:8, :] = ubuf_ref[t:t + 8, :]
        y = cb_ref[...] + cw_ref[0:1, :] * u2 + cw_ref[1:2, :] * u1 + cw_ref[2:3, :] * u
        y = gate_b * y
        for cc in range(CONV_WIDTH // LANES):
            sl = slice(cc * LANES, (cc + 1) * LANES)
            yv = y[:, sl]
            sq = yv * yv
            ms_lo = jnp.sum(jnp.where(lo, sq, 0.0), axis=-1, keepdims=True)
            ms_hi = jnp.sum(jnp.where(lo, 0.0, sq), axis=-1, keepdims=True)
            ms = jnp.where(lo, ms_lo, ms_hi) * (1.0 / (CONV_WIDTH // CONV_GROUPS))
            yc_ref[0, r0:r1, sl] = (yv * lax.rsqrt(ms + RMS_EPS) * og_ref[:, sl]).astype(BF16)


def _mixer_proj(x, mod, tab, win, qg, wq, kvg, wk, wvt, cw, cb, og):
    b, s, d = x.shape
    t = TOK_TILE
    nt = s // t
    const = lambda shape: pl.BlockSpec(shape, lambda bi, si: (0,) * len(shape))
    return pl.pallas_call(
        _proj_kernel,
        grid=(b, nt),
        in_specs=[pl.BlockSpec((1, t, d), lambda bi, si: (bi, si, 0)),
                  pl.BlockSpec((1, 6, d), lambda bi, si: (bi, 0, 0)),
                  pl.BlockSpec((1, t, LANES), lambda bi, si: (bi, si, 0)),
                  const(win.shape), const(qg.shape), const(wq.shape), const(kvg.shape),
                  const(wk.shape), const(wvt.shape), const(cw.shape), const(cb.shape), const(og.shape)],
        out_specs=[pl.BlockSpec((1, t, MLA_HEADS * HEAD_LANES), lambda bi, si: (bi, si, 0)),
                   pl.BlockSpec((1, t, MLA_HEADS * HEAD_LANES), lambda bi, si: (bi, si, 0)),
                   pl.BlockSpec((1, ATTN_WIDTH, t), lambda bi, si: (bi, 0, si)),
                   pl.BlockSpec((1, t, CONV_WIDTH), lambda bi, si: (bi, si, 0))],
        out_shape=[jax.ShapeDtypeStruct((b, s, MLA_HEADS * HEAD_LANES), BF16),
                   jax.ShapeDtypeStruct((b, s, MLA_HEADS * HEAD_LANES), BF16),
                   jax.ShapeDtypeStruct((b, ATTN_WIDTH, s), BF16),
                   jax.ShapeDtypeStruct((b, s, CONV_WIDTH), BF16)],
        scratch_shapes=[pltpu.VMEM((t + 8, CONV_WIDTH), F32)],
        compiler_params=pltpu.CompilerParams(dimension_semantics=("arbitrary", "arbitrary"),
                                             vmem_limit_bytes=56 * 1024 * 1024, flags=SCHED_FLAGS),
        name="mixer_proj",
    )(x, mod, tab, win, qg, wq, kvg, wk, wvt, cw, cb, og)


def _attn_kernel(q_ref, k_ref, vt_ref, g_ref, o_ref, s_ref, p_ref, ot_ref):
    seq = q_ref.shape[1]
    key_chunk = lax.broadcasted_iota(jnp.int32, (Q_TILE, Q_TILE), 0) // CHUNK
    qry_chunk = lax.broadcasted_iota(jnp.int32, (Q_TILE, Q_TILE), 1) // CHUNK
    allowed = key_chunk <= qry_chunk
    dn = (((1,), (1,)), ((), ()))
    for qi in range(seq // Q_TILE):
        q0 = qi * Q_TILE
        kmax = q0 + Q_TILE
        for hh in range(2):
            hl = slice(hh * HEAD_LANES, (hh + 1) * HEAD_LANES)
            buf = hh
            q = q_ref[0, q0:kmax, hl]
            if qi > 0:
                s_ref[buf, 0:q0, :] = lax.dot_general(k_ref[0, 0:q0, hl], q, dn, preferred_element_type=F32)
            sd = lax.dot_general(k_ref[0, q0:kmax, hl], q, dn, preferred_element_type=F32)
            s_ref[buf, q0:kmax, :] = jnp.where(allowed, sd, NEG)
            m = None
            for j in range(kmax // K_TILE):
                mj = jnp.max(s_ref[buf, j * K_TILE:(j + 1) * K_TILE, :], axis=0, keepdims=True)
                m = mj if m is None else jnp.maximum(m, mj)
            l = None
            for j in range(kmax // K_TILE):
                rows = slice(j * K_TILE, (j + 1) * K_TILE)
                p = jnp.exp2(s_ref[buf, rows, :] - m)
                lj = jnp.sum(p, axis=0, keepdims=True)
                l = lj if l is None else l + lj
                p_ref[buf, rows, :] = p.astype(BF16)
            o = jnp.dot(vt_ref[0, hh * V_HEAD_DIM:(hh + 1) * V_HEAD_DIM, 0:kmax], p_ref[buf, 0:kmax, :],
                        preferred_element_type=F32)
            o = o / l
            ms = jnp.mean(o * o, axis=0, keepdims=True)
            ot_ref[hh * V_HEAD_DIM:(hh + 1) * V_HEAD_DIM, :] = o * lax.rsqrt(ms + RMS_EPS)
        o_ref[0, q0:kmax, :] = (ot_ref[...].T * g_ref[...]).astype(BF16)


def _attention(q, k, vt, g):
    b, s, _ = q.shape
    pairs = MLA_HEADS // 2
    return pl.pallas_call(
        _attn_kernel,
        grid=(b, pairs),
        in_specs=[pl.BlockSpec((1, s, 2 * HEAD_LANES), lambda bi, pi: (bi, 0, pi)),
                  pl.BlockSpec((1, s, 2 * HEAD_LANES), lambda bi, pi: (bi, 0, pi)),
                  pl.BlockSpec((1, 2 * V_HEAD_DIM, s), lambda bi, pi: (bi, pi, 0)),
                  pl.BlockSpec((1, 2 * V_HEAD_DIM), lambda bi, pi: (0, pi))],
        out_specs=pl.BlockSpec((1, s, 2 * V_HEAD_DIM), lambda bi, pi: (bi, 0, pi)),
        out_shape=jax.ShapeDtypeStruct((b, s, ATTN_WIDTH), BF16),
        scratch_shapes=[pltpu.VMEM((2, s, Q_TILE), F32),
                        pltpu.VMEM((2, s, Q_TILE), BF16),
                        pltpu.VMEM((2 * V_HEAD_DIM, Q_TILE), F32)],
        compiler_params=pltpu.CompilerParams(dimension_semantics=("arbitrary",) * 2,
                                             vmem_limit_bytes=48 * 1024 * 1024, flags=SCHED_FLAGS),
        name="attention",
    )(q, k, vt, g)


def _ffn_kernel(x_ref, ya_ref, yc_ref, mod_ref, woa_ref, woc_ref, ln1g_ref, ln1b_ref, wup_ref,
                fcw_ref, fcb_ref, wdn_ref, ln2g_ref, ln2b_ref, o_ref, ubuf_ref, carry_ref, act_ref):
    t = TOK_TILE
    si = pl.program_id(1)
    gate_m = mod_ref[0, 2:3, :]
    shift_f = mod_ref[0, 3:4, :]
    scale_f = mod_ref[0, 4:5, :]
    gate_f = mod_ref[0, 5:6, :]

    @pl.when(si == 0)
    def _():
        carry_ref[...] = jnp.zeros(carry_ref.shape, F32)

    blocks = [(r, r + t // FFN_ROW_BLOCKS) for r in range(0, t, t // FFN_ROW_BLOCKS)]
    x1s, hs = [], []
    for r0, r1 in blocks:
        mix = (jnp.dot(ya_ref[0, r0:r1, :], woa_ref[...], preferred_element_type=F32)
               + jnp.dot(yc_ref[0, r0:r1, :], woc_ref[...], preferred_element_type=F32))
        x1 = _layer_norm(DEEPNORM_ALPHA * x_ref[0, r0:r1, :] + gate_m * mix) * ln1g_ref[...] + ln1b_ref[...]
        x1s.append(x1)
        hs.append((_layer_norm(x1) * (1.0 + scale_f) + shift_f).astype(BF16))

    def conv_cols(slot, cols, h, r0, r1):
        u = jnp.dot(h, wup_ref[:, cols], preferred_element_type=F32)
        if r0 == 0:
            ubuf_ref[slot, 0:8, :] = carry_ref[:, cols]
        ubuf_ref[slot, r0 + 8:r1 + 8, :] = u
        u1 = ubuf_ref[slot, r0 + 7:r1 + 7, :]
        u2 = ubuf_ref[slot, r0 + 6:r1 + 6, :]
        if r1 == t:
            carry_ref[:, cols] = ubuf_ref[slot, t:t + 8, :]
        return fcb_ref[:, cols] + fcw_ref[0:1, cols] * u2 + fcw_ref[1:2, cols] * u1 + fcw_ref[2:3, cols] * u

    for c in range(N_FF_CHUNKS):
        for (r0, r1), h in zip(blocks, hs):
            g = conv_cols(0, slice(c * FF_CHUNK, (c + 1) * FF_CHUNK), h, r0, r1)
            v = conv_cols(1, slice(D_FF + c * FF_CHUNK, D_FF + (c + 1) * FF_CHUNK), h, r0, r1)
            act_ref[r0:r1, c * FF_CHUNK:(c + 1) * FF_CHUNK] = (g * _sigmoid(g) * v).astype(BF16)

    for (r0, r1), x1 in zip(blocks, x1s):
        ff = jnp.dot(act_ref[r0:r1, :], wdn_ref[...], preferred_element_type=F32)
        o_ref[0, r0:r1, :] = (_layer_norm(DEEPNORM_ALPHA * x1 + gate_f * ff) * ln2g_ref[...]
                              + ln2b_ref[...])


def _out_ffn(x, ya, yc, mod, woa, woc, ln1g, ln1b, wup, fcw, fcb, wdn, ln2g, ln2b):
    b, s, d = x.shape
    t = TOK_TILE
    const = lambda shape: pl.BlockSpec(shape, lambda bi, si: (0,) * len(shape),
                                       pipeline_mode=pl.Buffered(1))
    tok = lambda width: pl.BlockSpec((1, t, width), lambda bi, si: (bi, si, 0))
    return pl.pallas_call(
        _ffn_kernel,
        grid=(b, s // t),
        in_specs=[tok(d), tok(ATTN_WIDTH), tok(CONV_WIDTH),
                  pl.BlockSpec((1, 6, d), lambda bi, si: (bi, 0, 0)),
                  const(woa.shape), const(woc.shape), const(ln1g.shape), const(ln1b.shape),
                  const(wup.shape), const(fcw.shape), const(fcb.shape), const(wdn.shape),
                  const(ln2g.shape), const(ln2b.shape)],
        out_specs=tok(d),
        out_shape=jax.ShapeDtypeStruct((b, s, d), F32),
        scratch_shapes=[pltpu.VMEM((2, t + 8, FF_CHUNK), F32),
                        pltpu.VMEM((8, 2 * D_FF), F32),
                        pltpu.VMEM((t, D_FF), BF16)],
        compiler_params=pltpu.CompilerParams(dimension_semantics=("arbitrary", "arbitrary"),
                                             vmem_limit_bytes=56 * 1024 * 1024, flags=SCHED_FLAGS),
        name="out_ffn",
    )(x, ya, yc, mod, woa, woc, ln1g, ln1b, wup, fcw, fcb, wdn, ln2g, ln2b)


def _swap_halves(w):
    half = w.shape[-1] // 2
    return jnp.concatenate([w[..., half:], w[..., :half]], axis=-1)


def _prep_in_proj(w_in):
    c0 = Q_LORA_RANK + KV_LORA_RANK
    kr = w_in[:, c0:c0 + QK_ROPE_DIM]
    krs = _swap_halves(kr)
    return jnp.concatenate([w_in[:, :c0], kr, krs, kr, krs, w_in[:, c0 + QK_ROPE_DIM:]], axis=1).astype(BF16)


def _prep_q_up(w_q_up):
    w = w_q_up.reshape(Q_LORA_RANK, MLA_HEADS, QK_NOPE_DIM + QK_ROPE_DIM)
    rope = w[..., QK_NOPE_DIM:]
    w = jnp.concatenate([w[..., :QK_NOPE_DIM], rope, _swap_halves(rope)], axis=-1)
    return w.reshape(Q_LORA_RANK, MLA_HEADS * HEAD_LANES).astype(BF16)


def _prep_kv_up(w_kv_up):
    w = w_kv_up.reshape(KV_LORA_RANK, MLA_HEADS, QK_NOPE_DIM + V_HEAD_DIM)
    knope = w[..., :QK_NOPE_DIM]
    wk = jnp.concatenate([knope, jnp.zeros_like(knope)], axis=-1).reshape(KV_LORA_RANK, MLA_HEADS * HEAD_LANES)
    wvt = w[..., QK_NOPE_DIM:].reshape(KV_LORA_RANK, ATTN_WIDTH).T
    return wk.astype(BF16), wvt.astype(BF16)


def kernel(x, c, positions, w_ada, b_ada, w_in, q_norm_g, w_q_up, kv_norm_g, w_kv_up, conv_w, conv_b,
           out_norm_g, w_out, ln1_g, ln1_b, w_up, ffn_conv_w, ffn_conv_b, w_down, ln2_g, ln2_b):
    b, s, d = x.shape
    depth = w_ada.shape[0]
    tab = _rope_table(positions)
    for l in range(depth):
        mod = _adaln_mod(c, w_ada[l], b_ada[l]).reshape(b, 6, d)
        wk, wvt = _prep_kv_up(w_kv_up[l])
        q, k, vt, yc = _mixer_proj(
            x, mod, tab, _prep_in_proj(w_in[l]), q_norm_g[l].reshape(1, -1), _prep_q_up(w_q_up[l]),
            kv_norm_g[l].reshape(1, -1), wk, wvt, conv_w[l], conv_b[l].reshape(1, -1),
            out_norm_g[l, ATTN_WIDTH:].reshape(1, -1))
        ya = _attention(q, k, vt, out_norm_g[l, :ATTN_WIDTH].reshape(1, -1))
        x = _out_ffn(
            x, ya, yc, mod, w_out[l, :ATTN_WIDTH].astype(BF16), w_out[l, ATTN_WIDTH:].astype(BF16),
            ln1_g[l].reshape(1, -1), ln1_b[l].reshape(1, -1), w_up[l].astype(BF16),
            ffn_conv_w[l], ffn_conv_b[l].reshape(1, -1),
            w_down[l].astype(BF16), ln2_g[l].reshape(1, -1), ln2_b[l].reshape(1, -1))
    return x
```

```python
import jax
import jax.numpy as jnp
from jax import lax
from jax.experimental import pallas as pl
from jax.experimental.pallas import tpu as pltpu

D_MODEL = 1024
SEQ = 2048
CHUNK = 64
MLA_HEADS = 8
QK_NOPE_DIM = 64
QK_ROPE_DIM = 32
V_HEAD_DIM = 64
Q_LORA_RANK = 256
KV_LORA_RANK = 128
ATTN_WIDTH = MLA_HEADS * V_HEAD_DIM
CONV_WIDTH = D_MODEL - ATTN_WIDTH
CONV_GROUPS = 8
D_FF = 2816
ROPE_THETA = 10000.0
RMS_EPS = 1e-6
LN_EPS = 1e-5
DEEPNORM_ALPHA = 2.0 ** 0.25

LANES = 128
HEAD_LANES = 128
IN_EXT = Q_LORA_RANK + KV_LORA_RANK + LANES + 3 * CONV_WIDTH
TOK_TILE = 512
Q_TILE = 512
K_TILE = 256
FF_CHUNK = 256
N_FF_CHUNKS = D_FF // FF_CHUNK
FFN_ROW_BLOCKS = 2
PROJ_ROW_BLOCKS = 2
NEG = -1e30
Q_SCALE = (QK_NOPE_DIM + QK_ROPE_DIM) ** -0.5 * 1.4426950408889634

F32 = jnp.float32
BF16 = jnp.bfloat16
SCHED_FLAGS = None


def _layer_norm(x):
    mu = jnp.mean(x, axis=-1, keepdims=True)
    xc = x - mu
    var = jnp.mean(xc * xc, axis=-1, keepdims=True)
    return xc * lax.rsqrt(var + LN_EPS)


def _rms(x):
    return x * lax.rsqrt(jnp.mean(x * x, axis=-1, keepdims=True) + RMS_EPS)


def _sigmoid(x):
    return 1.0 / (1.0 + jnp.exp(-x))


def _split3(x):
    hi = x.astype(BF16)
    r1 = x - hi.astype(F32)
    mid = r1.astype(BF16)
    lo = (r1 - mid.astype(F32)).astype(BF16)
    return hi, mid, lo


def _rope_kernel(pos_ref, freq_ref, e_ref, tab_ref):
    ang = pos_ref[...] * freq_ref[...]
    pieces = _split3(jnp.cos(ang)) + _split3(jnp.sin(ang))
    acc = None
    for i, piece in enumerate(pieces):
        part = jnp.dot(piece, e_ref[i // 3], preferred_element_type=F32)
        acc = part if acc is None else acc + part
    tab_ref[...] = acc


def _rope_expansion():
    half = QK_ROPE_DIM // 2
    src = jnp.arange(LANES)[:, None]
    dst = jnp.arange(8 * LANES)[None, :]
    same = (src // half == dst // LANES) & (src % half == dst % half)
    group = (dst % LANES) // half % 4
    e_cos = jnp.where(same & (group < 2), 1.0, 0.0)
    e_sin = jnp.where(same & (group == 2), -1.0, jnp.where(same & (group == 3), 1.0, 0.0))
    return jnp.stack([e_cos, e_sin]).astype(BF16)


def _rope_table(positions):
    b, s = positions.shape
    half = QK_ROPE_DIM // 2
    rows = b * s * half // LANES
    blk = 1024
    inv_freq = ROPE_THETA ** (-jnp.arange(0, QK_ROPE_DIM, 2, dtype=F32) / QK_ROPE_DIM)
    pos = jnp.repeat(positions.astype(F32), half, axis=-1).reshape(rows, LANES)
    freq = jnp.tile(inv_freq, LANES // half).reshape(1, LANES)
    tab = pl.pallas_call(
        _rope_kernel,
        grid=(rows // blk,),
        in_specs=[pl.BlockSpec((blk, LANES), lambda i: (i, 0)),
                  pl.BlockSpec((1, LANES), lambda i: (0, 0)),
                  pl.BlockSpec((2, LANES, 8 * LANES), lambda i: (0, 0, 0))],
        out_specs=pl.BlockSpec((blk, 8 * LANES), lambda i: (i, 0)),
        out_shape=jax.ShapeDtypeStruct((rows, 8 * LANES), F32),
        compiler_params=pltpu.CompilerParams(dimension_semantics=("arbitrary",),
                                             vmem_limit_bytes=32 * 1024 * 1024),
        name="rope_table",
    )(pos, freq, _rope_expansion())
    return tab.reshape(b, s, LANES)


def _mod_kernel(c_ref, w_ref, b_ref, o_ref):
    c = c_ref[...]
    act = c * _sigmoid(c)
    o_ref[...] = jnp.dot(act, w_ref[...], preferred_element_type=F32,
                         precision=lax.Precision.HIGHEST) + b_ref[...]


def _adaln_mod(c, w_ada, b_ada):
    b, d = c.shape
    n = w_ada.shape[1]
    return pl.pallas_call(
        _mod_kernel,
        grid=(n // d,),
        in_specs=[pl.BlockSpec((b, d), lambda j: (0, 0)),
                  pl.BlockSpec((d, d), lambda j: (0, j)),
                  pl.BlockSpec((1, d), lambda j: (0, j))],
        out_specs=pl.BlockSpec((b, d), lambda j: (0, j)),
        out_shape=jax.ShapeDtypeStruct((b, n), F32),
        compiler_params=pltpu.CompilerParams(dimension_semantics=("arbitrary",),
                                             vmem_limit_bytes=32 * 1024 * 1024),
        name="adaln_mod",
    )(c, w_ada, b_ada.reshape(1, n))


def _proj_kernel(x_ref, mod_ref, tab_ref, win_ref, qg_ref, wq_ref, kvg_ref, wk_ref, wvt_ref,
                 cw_ref, cb_ref, og_ref, q_ref, k_ref, vt_ref, yc_ref, ubuf_ref):
    t = TOK_TILE
    si = pl.program_id(1)
    shift = mod_ref[0, 0:1, :]
    scale = mod_ref[0, 1:2, :]
    lane = lax.broadcasted_iota(jnp.int32, (1, LANES), 1)
    is_nope = lane < QK_NOPE_DIM
    lo = lane < (CONV_WIDTH // CONV_GROUPS)
    c0 = Q_LORA_RANK
    c1 = c0 + KV_LORA_RANK
    c2 = c1 + LANES

    @pl.when(si == 0)
    def _():
        ubuf_ref[0:8, :] = jnp.zeros((8, CONV_WIDTH), F32)

    for r0 in range(0, t, t // PROJ_ROW_BLOCKS):
        r1 = r0 + t // PROJ_ROW_BLOCKS
        h = (_layer_norm(x_ref[0, r0:r1, :]) * (1.0 + scale) + shift).astype(BF16)
        proj = jnp.dot(h, win_ref[...], preferred_element_type=F32)
        tab = tab_ref[0, r0:r1, :]

        cq = (_rms(proj[:, 0:Q_LORA_RANK]) * qg_ref[...]).astype(BF16)
        q = jnp.dot(cq, wq_ref[...], preferred_element_type=F32)
        q_tab = jnp.where(is_nope, Q_SCALE, Q_SCALE * tab)
        for hh in range(MLA_HEADS):
            sl = slice(hh * HEAD_LANES, (hh + 1) * HEAD_LANES)
            q_ref[0, r0:r1, sl] = (q[:, sl] * q_tab).astype(BF16)

        ckv = (_rms(proj[:, c0:c0 + KV_LORA_RANK]) * kvg_ref[...]).astype(BF16)
        knope = jnp.dot(ckv, wk_ref[...], preferred_element_type=F32)
        r = proj[:, c1:c1 + LANES] * tab
        kf = r + pltpu.roll(r, QK_ROPE_DIM, axis=1)
        kf = jnp.where(is_nope, 0.0, kf)
        for hh in range(MLA_HEADS):
            sl = slice(hh * HEAD_LANES, (hh + 1) * HEAD_LANES)
            k_ref[0, r0:r1, sl] = (knope[:, sl] + kf).astype(BF16)
        vt_ref[0, :, r0:r1] = lax.dot_general(wvt_ref[...], ckv, (((1,), (1,)), ((), ())),
                                              preferred_element_type=F32).astype(BF16)

        gate_b = proj[:, c2:c2 + CONV_WIDTH]
        u = proj[:, c2 + CONV_WIDTH:c2 + 2 * CONV_WIDTH] * proj[:, c2 + 2 * CONV_WIDTH:c2 + 3 * CONV_WIDTH]
        ubuf_ref[r0 + 8:r1 + 8, :] = u
        u1 = ubuf_ref[r0 + 7:r1 + 7, :]
        u2 = ubuf_ref[r0 + 6:r1 + 6, :]
        if r1 == t:
            ubuf_ref[0:8, :] = ubuf_ref[t:t + 8, :]
        y = cb_ref[...] + cw_ref[0:1, :] * u2 + cw_ref[1:2, :] * u1 + cw_ref[2:3, :] * u
        y = gate_b * y
        for cc in range(CONV_WIDTH // LANES):
            sl = slice(cc * LANES, (cc + 1) * LANES)
            yv = y[:, sl]
            sq = yv * yv
            ms_lo = jnp.sum(jnp.where(lo, sq, 0.0), axis=-1, keepdims=True)
            ms_hi = jnp.sum(jnp.where(lo, 0.0, sq), axis=-1, keepdims=True)
            ms = jnp.where(lo, ms_lo, ms_hi) * (1.0 / (CONV_WIDTH // CONV_GROUPS))
            yc_ref[0, r0:r1, sl] = (yv * lax.rsqrt(ms + RMS_EPS) * og_ref[:, sl]).astype(BF16)


def _mixer_proj(x, mod, tab, win, qg, wq, kvg, wk, wvt, cw, cb, og):
    b, s, d = x.shape
    t = TOK_TILE
    nt = s // t
    const = lambda shape: pl.BlockSpec(shape, lambda bi, si: (0,) * len(shape))
    return pl.pallas_call(
        _proj_kernel,
        grid=(b, nt),
        in_specs=[pl.BlockSpec((1, t, d), lambda bi, si: (bi, si, 0)),
                  pl.BlockSpec((1, 6, d), lambda bi, si: (bi, 0, 0)),
                  pl.BlockSpec((1, t, LANES), lambda bi, si: (bi, si, 0)),
                  const(win.shape), const(qg.shape), const(wq.shape), const(kvg.shape),
                  const(wk.shape), const(wvt.shape), const(cw.shape), const(cb.shape), const(og.shape)],
        out_specs=[pl.BlockSpec((1, t, MLA_HEADS * HEAD_LANES), lambda bi, si: (bi, si, 0)),
                   pl.BlockSpec((1, t, MLA_HEADS * HEAD_LANES), lambda bi, si: (bi, si, 0)),
                   pl.BlockSpec((1, ATTN_WIDTH, t), lambda bi, si: (bi, 0, si)),
                   pl.BlockSpec((1, t, CONV_WIDTH), lambda bi, si: (bi, si, 0))],
        out_shape=[jax.ShapeDtypeStruct((b, s, MLA_HEADS * HEAD_LANES), BF16),
                   jax.ShapeDtypeStruct((b, s, MLA_HEADS * HEAD_LANES), BF16),
                   jax.ShapeDtypeStruct((b, ATTN_WIDTH, s), BF16),
                   jax.ShapeDtypeStruct((b, s, CONV_WIDTH), BF16)],
        scratch_shapes=[pltpu.VMEM((t + 8, CONV_WIDTH), F32)],
        compiler_params=pltpu.CompilerParams(dimension_semantics=("arbitrary", "arbitrary"),
                                             vmem_limit_bytes=56 * 1024 * 1024, flags=SCHED_FLAGS),
        name="mixer_proj",
    )(x, mod, tab, win, qg, wq, kvg, wk, wvt, cw, cb, og)


def _attn_kernel(q_ref, k_ref, vt_ref, g_ref, o_ref, s_ref, ot_ref):
    seq = q_ref.shape[1]
    qry_chunk = lax.broadcasted_iota(jnp.int32, (1, Q_TILE), 1) // CHUNK
    ones = jnp.ones((16, K_TILE), BF16)
    dn = (((1,), (1,)), ((), ()))

    def scores(qi, hh):
        q0 = qi * Q_TILE
        hl = slice(hh * HEAD_LANES, (hh + 1) * HEAD_LANES)
        q = q_ref[0, q0:q0 + Q_TILE, hl]
        if qi > 0:
            s_ref[hh, 0:q0, :] = lax.dot_general(k_ref[0, 0:q0, hl], q, dn, preferred_element_type=F32)
        sd = lax.dot_general(k_ref[0, q0:q0 + Q_TILE, hl], q, dn, preferred_element_type=F32)
        for kc in range(Q_TILE // CHUNK):
            rows = slice(kc * CHUNK, (kc + 1) * CHUNK)
            s_ref[hh, q0 + kc * CHUNK:q0 + (kc + 1) * CHUNK, :] = jnp.where(qry_chunk >= kc, sd[rows], NEG)

    def softmax_pv(qi, hh):
        kmax = (qi + 1) * Q_TILE
        m = None
        for j in range(kmax // K_TILE):
            mj = jnp.max(s_ref[hh, j * K_TILE:(j + 1) * K_TILE, :], axis=0, keepdims=True)
            m = mj if m is None else jnp.maximum(m, mj)
        o = None
        for j in range(kmax // K_TILE):
            rows = slice(j * K_TILE, (j + 1) * K_TILE)
            p = jnp.exp2(s_ref[hh, rows, :] - m).astype(BF16)
            lhs = jnp.concatenate([vt_ref[0, hh * V_HEAD_DIM:(hh + 1) * V_HEAD_DIM, rows], ones], axis=0)
            oj = jnp.dot(lhs, p, preferred_element_type=F32)
            o = oj if o is None else o + oj
        o = o[0:V_HEAD_DIM] / o[V_HEAD_DIM:V_HEAD_DIM + 1]
        ms = jnp.mean(o * o, axis=0, keepdims=True)
        ot_ref[hh * V_HEAD_DIM:(hh + 1) * V_HEAD_DIM, :] = o * lax.rsqrt(ms + RMS_EPS)

    chains = [(qi, hh) for qi in range(seq // Q_TILE) for hh in range(2)]
    scores(*chains[0])
    for n, (qi, hh) in enumerate(chains):
        if n + 1 < len(chains):
            scores(*chains[n + 1])
        softmax_pv(qi, hh)
        if hh == 1:
            o_ref[0, qi * Q_TILE:(qi + 1) * Q_TILE, :] = (ot_ref[...].T * g_ref[...]).astype(BF16)


def _attention(q, k, vt, g):
    b, s, _ = q.shape
    pairs = MLA_HEADS // 2
    return pl.pallas_call(
        _attn_kernel,
        grid=(b, pairs),
        in_specs=[pl.BlockSpec((1, s, 2 * HEAD_LANES), lambda bi, pi: (bi, 0, pi)),
                  pl.BlockSpec((1, s, 2 * HEAD_LANES), lambda bi, pi: (bi, 0, pi)),
                  pl.BlockSpec((1, 2 * V_HEAD_DIM, s), lambda bi, pi: (bi, pi, 0)),
                  pl.BlockSpec((1, 2 * V_HEAD_DIM), lambda bi, pi: (0, pi))],
        out_specs=pl.BlockSpec((1, s, 2 * V_HEAD_DIM), lambda bi, pi: (bi, 0, pi)),
        out_shape=jax.ShapeDtypeStruct((b, s, ATTN_WIDTH), BF16),
        scratch_shapes=[pltpu.VMEM((2, s, Q_TILE), F32),
                        pltpu.VMEM((2 * V_HEAD_DIM, Q_TILE), F32)],
        compiler_params=pltpu.CompilerParams(dimension_semantics=("arbitrary",) * 2,
                                             vmem_limit_bytes=48 * 1024 * 1024, flags=SCHED_FLAGS),
        name="attention",
    )(q, k, vt, g)


def _ffn_kernel(x_ref, ya_ref, yc_ref, mod_ref, woa_ref, woc_ref, ln1g_ref, ln1b_ref, wup_ref,
                fcw_ref, fcb_ref, wdn_ref, ln2g_ref, ln2b_ref, o_ref, ubuf_ref, carry_ref, act_ref):
    t = TOK_TILE
    si = pl.program_id(1)
    gate_m = mod_ref[0, 2:3, :]
    shift_f = mod_ref[0, 3:4, :]
    scale_f = mod_ref[0, 4:5, :]
    gate_f = mod_ref[0, 5:6, :]

    @pl.when(si == 0)
    def _():
        carry_ref[...] = jnp.zeros(carry_ref.shape, F32)

    blocks = [(r, r + t // FFN_ROW_BLOCKS) for r in range(0, t, t // FFN_ROW_BLOCKS)]
    x1s, hs = [], []
    for r0, r1 in blocks:
        mix = (jnp.dot(ya_ref[0, r0:r1, :], woa_ref[...], preferred_element_type=F32)
               + jnp.dot(yc_ref[0, r0:r1, :], woc_ref[...], preferred_element_type=F32))
        x1 = _layer_norm(DEEPNORM_ALPHA * x_ref[0, r0:r1, :] + gate_m * mix) * ln1g_ref[...] + ln1b_ref[...]
        x1s.append(x1)
        hs.append((_layer_norm(x1) * (1.0 + scale_f) + shift_f).astype(BF16))

    def conv_cols(slot, cols, h, r0, r1):
        u = jnp.dot(h, wup_ref[:, cols], preferred_element_type=F32)
        if r0 == 0:
            ubuf_ref[slot, 0:8, :] = carry_ref[:, cols]
        ubuf_ref[slot, r0 + 8:r1 + 8, :] = u
        u1 = ubuf_ref[slot, r0 + 7:r1 + 7, :]
        u2 = ubuf_ref[slot, r0 + 6:r1 + 6, :]
        if r1 == t:
            carry_ref[:, cols] = ubuf_ref[slot, t:t + 8, :]
        return fcb_ref[:, cols] + fcw_ref[0:1, cols] * u2 + fcw_ref[1:2, cols] * u1 + fcw_ref[2:3, cols] * u

    for c in range(N_FF_CHUNKS):
        for (r0, r1), h in zip(blocks, hs):
            g = conv_cols(0, slice(c * FF_CHUNK, (c + 1) * FF_CHUNK), h, r0, r1)
            v = conv_cols(1, slice(D_FF + c * FF_CHUNK, D_FF + (c + 1) * FF_CHUNK), h, r0, r1)
            act_ref[r0:r1, c * FF_CHUNK:(c + 1) * FF_CHUNK] = (g * _sigmoid(g) * v).astype(BF16)

    for (r0, r1), x1 in zip(blocks, x1s):
        ff = jnp.dot(act_ref[r0:r1, :], wdn_ref[...], preferred_element_type=F32)
        o_ref[0, r0:r1, :] = (_layer_norm(DEEPNORM_ALPHA * x1 + gate_f * ff) * ln2g_ref[...]
                              + ln2b_ref[...])


def _out_ffn(x, ya, yc, mod, woa, woc, ln1g, ln1b, wup, fcw, fcb, wdn, ln2g, ln2b):
    b, s, d = x.shape
    t = TOK_TILE
    const = lambda shape: pl.BlockSpec(shape, lambda bi, si: (0,) * len(shape),
                                       pipeline_mode=pl.Buffered(1))
    tok = lambda width: pl.BlockSpec((1, t, width), lambda bi, si: (bi, si, 0))
    return pl.pallas_call(
        _ffn_kernel,
        grid=(b, s // t),
        in_specs=[tok(d), tok(ATTN_WIDTH), tok(CONV_WIDTH),
                  pl.BlockSpec((1, 6, d), lambda bi, si: (bi, 0, 0)),
                  const(woa.shape), const(woc.shape), const(ln1g.shape), const(ln1b.shape),
                  const(wup.shape), const(fcw.shape), const(fcb.shape), const(wdn.shape),
                  const(ln2g.shape), const(ln2b.shape)],
        out_specs=tok(d),
        out_shape=jax.ShapeDtypeStruct((b, s, d), F32),
        scratch_shapes=[pltpu.VMEM((2, t + 8, FF_CHUNK), F32),
                        pltpu.VMEM((8, 2 * D_FF), F32),
                        pltpu.VMEM((t, D_FF), BF16)],
        compiler_params=pltpu.CompilerParams(dimension_semantics=("arbitrary", "arbitrary"),
                                             vmem_limit_bytes=56 * 1024 * 1024, flags=SCHED_FLAGS),
        name="out_ffn",
    )(x, ya, yc, mod, woa, woc, ln1g, ln1b, wup, fcw, fcb, wdn, ln2g, ln2b)


def _swap_halves(w):
    half = w.shape[-1] // 2
    return jnp.concatenate([w[..., half:], w[..., :half]], axis=-1)


def _prep_in_proj(w_in):
    c0 = Q_LORA_RANK + KV_LORA_RANK
    kr = w_in[:, c0:c0 + QK_ROPE_DIM]
    krs = _swap_halves(kr)
    return jnp.concatenate([w_in[:, :c0], kr, krs, kr, krs, w_in[:, c0 + QK_ROPE_DIM:]], axis=1).astype(BF16)


def _prep_q_up(w_q_up):
    w = w_q_up.reshape(Q_LORA_RANK, MLA_HEADS, QK_NOPE_DIM + QK_ROPE_DIM)
    rope = w[..., QK_NOPE_DIM:]
    w = jnp.concatenate([w[..., :QK_NOPE_DIM], rope, _swap_halves(rope)], axis=-1)
    return w.reshape(Q_LORA_RANK, MLA_HEADS * HEAD_LANES).astype(BF16)


def _prep_kv_up(w_kv_up):
    w = w_kv_up.reshape(KV_LORA_RANK, MLA_HEADS, QK_NOPE_DIM + V_HEAD_DIM)
    knope = w[..., :QK_NOPE_DIM]
    wk = jnp.concatenate([knope, jnp.zeros_like(knope)], axis=-1).reshape(KV_LORA_RANK, MLA_HEADS * HEAD_LANES)
    wvt = w[..., QK_NOPE_DIM:].reshape(KV_LORA_RANK, ATTN_WIDTH).T
    return wk.astype(BF16), wvt.astype(BF16)


def kernel(x, c, positions, w_ada, b_ada, w_in, q_norm_g, w_q_up, kv_norm_g, w_kv_up, conv_w, conv_b,
           out_norm_g, w_out, ln1_g, ln1_b, w_up, ffn_conv_w, ffn_conv_b, w_down, ln2_g, ln2_b):
    b, s, d = x.shape
    depth = w_ada.shape[0]
    tab = _rope_table(positions)
    for l in range(depth):
        mod = _adaln_mod(c, w_ada[l], b_ada[l]).reshape(b, 6, d)
        wk, wvt = _prep_kv_up(w_kv_up[l])
        q, k, vt, yc = _mixer_proj(
            x, mod, tab, _prep_in_proj(w_in[l]), q_norm_g[l].reshape(1, -1), _prep_q_up(w_q_up[l]),
            kv_norm_g[l].reshape(1, -1), wk, wvt, conv_w[l], conv_b[l].reshape(1, -1),
            out_norm_g[l, ATTN_WIDTH:].reshape(1, -1))
        ya = _attention(q, k, vt, out_norm_g[l, :ATTN_WIDTH].reshape(1, -1))
        x = _out_ffn(
            x, ya, yc, mod, w_out[l, :ATTN_WIDTH].astype(BF16), w_out[l, ATTN_WIDTH:].astype(BF16),
            ln1_g[l].reshape(1, -1), ln1_b[l].reshape(1, -1), w_up[l].astype(BF16),
            ffn_conv_w[l], ffn_conv_b[l].reshape(1, -1),
            w_down[l].astype(BF16), ln2_g[l].reshape(1, -1), ln2_b[l].reshape(1, -1))
    return x
```

```python
import jax
import jax.numpy as jnp
from jax import lax
from jax.experimental import pallas as pl
from jax.experimental.pallas import tpu as pltpu

D_MODEL = 1024
SEQ = 2048
CHUNK = 64
MLA_HEADS = 8
QK_NOPE_DIM = 64
QK_ROPE_DIM = 32
V_HEAD_DIM = 64
Q_LORA_RANK = 256
KV_LORA_RANK = 128
ATTN_WIDTH = MLA_HEADS * V_HEAD_DIM
CONV_WIDTH = D_MODEL - ATTN_WIDTH
CONV_GROUPS = 8
D_FF = 2816
ROPE_THETA = 10000.0
RMS_EPS = 1e-6
LN_EPS = 1e-5
DEEPNORM_ALPHA = 2.0 ** 0.25

LANES = 128
HEAD_LANES = 128
IN_EXT = Q_LORA_RANK + KV_LORA_RANK + LANES + 3 * CONV_WIDTH
TOK_TILE = 512
Q_TILE = 512
K_TILE = 256
FF_CHUNK = 256
N_FF_CHUNKS = D_FF // FF_CHUNK
FFN_ROW_BLOCKS = 2
PROJ_ROW_BLOCKS = 2
NEG = -1e30
Q_SCALE = (QK_NOPE_DIM + QK_ROPE_DIM) ** -0.5 * 1.4426950408889634

F32 = jnp.float32
BF16 = jnp.bfloat16
SCHED_FLAGS = None


def _layer_norm(x):
    mu = jnp.mean(x, axis=-1, keepdims=True)
    xc = x - mu
    var = jnp.mean(xc * xc, axis=-1, keepdims=True)
    return xc * lax.rsqrt(var + LN_EPS)


def _rms(x):
    return x * lax.rsqrt(jnp.mean(x * x, axis=-1, keepdims=True) + RMS_EPS)


def _sigmoid(x):
    return 1.0 / (1.0 + jnp.exp(-x))


def _split3(x):
    hi = x.astype(BF16)
    r1 = x - hi.astype(F32)
    mid = r1.astype(BF16)
    lo = (r1 - mid.astype(F32)).astype(BF16)
    return hi, mid, lo


def _rope_kernel(pos_ref, freq_ref, e_ref, tab_ref):
    ang = pos_ref[...] * freq_ref[...]
    pieces = _split3(jnp.cos(ang)) + _split3(jnp.sin(ang))
    acc = None
    for i, piece in enumerate(pieces):
        part = jnp.dot(piece, e_ref[i // 3], preferred_element_type=F32)
        acc = part if acc is None else acc + part
    tab_ref[...] = acc


def _rope_expansion():
    half = QK_ROPE_DIM // 2
    src = jnp.arange(LANES)[:, None]
    dst = jnp.arange(8 * LANES)[None, :]
    same = (src // half == dst // LANES) & (src % half == dst % half)
    group = (dst % LANES) // half % 4
    e_cos = jnp.where(same & (group < 2), 1.0, 0.0)
    e_sin = jnp.where(same & (group == 2), -1.0, jnp.where(same & (group == 3), 1.0, 0.0))
    return jnp.stack([e_cos, e_sin]).astype(BF16)


def _rope_table(positions):
    b, s = positions.shape
    half = QK_ROPE_DIM // 2
    rows = b * s * half // LANES
    blk = 1024
    inv_freq = ROPE_THETA ** (-jnp.arange(0, QK_ROPE_DIM, 2, dtype=F32) / QK_ROPE_DIM)
    pos = jnp.repeat(positions.astype(F32), half, axis=-1).reshape(rows, LANES)
    freq = jnp.tile(inv_freq, LANES // half).reshape(1, LANES)
    tab = pl.pallas_call(
        _rope_kernel,
        grid=(rows // blk,),
        in_specs=[pl.BlockSpec((blk, LANES), lambda i: (i, 0)),
                  pl.BlockSpec((1, LANES), lambda i: (0, 0)),
                  pl.BlockSpec((2, LANES, 8 * LANES), lambda i: (0, 0, 0))],
        out_specs=pl.BlockSpec((blk, 8 * LANES), lambda i: (i, 0)),
        out_shape=jax.ShapeDtypeStruct((rows, 8 * LANES), F32),
        compiler_params=pltpu.CompilerParams(dimension_semantics=("arbitrary",),
                                             vmem_limit_bytes=32 * 1024 * 1024),
        name="rope_table",
    )(pos, freq, _rope_expansion())
    return tab.reshape(b, s, LANES)


def _mod_kernel(c_ref, w_ref, b_ref, o_ref):
    c = c_ref[...]
    act = c * _sigmoid(c)
    o_ref[...] = jnp.dot(act, w_ref[...], preferred_element_type=F32,
                         precision=lax.Precision.HIGHEST) + b_ref[...]


def _adaln_mod(c, w_ada, b_ada):
    b, d = c.shape
    n = w_ada.shape[1]
    return pl.pallas_call(
        _mod_kernel,
        grid=(n // d,),
        in_specs=[pl.BlockSpec((b, d), lambda j: (0, 0)),
                  pl.BlockSpec((d, d), lambda j: (0, j)),
                  pl.BlockSpec((1, d), lambda j: (0, j))],
        out_specs=pl.BlockSpec((b, d), lambda j: (0, j)),
        out_shape=jax.ShapeDtypeStruct((b, n), F32),
        compiler_params=pltpu.CompilerParams(dimension_semantics=("arbitrary",),
                                             vmem_limit_bytes=32 * 1024 * 1024),
        name="adaln_mod",
    )(c, w_ada, b_ada.reshape(1, n))


def _proj_kernel(x_ref, mod_ref, tab_ref, win_ref, qg_ref, wq_ref, kvg_ref, wk_ref, wvt_ref,
                 cw_ref, cb_ref, og_ref, q_ref, k_ref, vt_ref, yc_ref, carry_ref):
    t = TOK_TILE
    si = pl.program_id(1)
    shift = mod_ref[0, 0:1, :]
    scale = mod_ref[0, 1:2, :]
    lane = lax.broadcasted_iota(jnp.int32, (1, LANES), 1)
    is_nope = lane < QK_NOPE_DIM
    lo = lane < (CONV_WIDTH // CONV_GROUPS)
    c0 = Q_LORA_RANK
    c1 = c0 + KV_LORA_RANK
    c2 = c1 + LANES

    @pl.when(si == 0)
    def _():
        carry_ref[...] = jnp.zeros((8, CONV_WIDTH), F32)

    prev = carry_ref[...]
    for r0 in range(0, t, t // PROJ_ROW_BLOCKS):
        r1 = r0 + t // PROJ_ROW_BLOCKS
        h = (_layer_norm(x_ref[0, r0:r1, :]) * (1.0 + scale) + shift).astype(BF16)
        proj = jnp.dot(h, win_ref[...], preferred_element_type=F32)
        tab = tab_ref[0, r0:r1, :]

        cq = (_rms(proj[:, 0:Q_LORA_RANK]) * qg_ref[...]).astype(BF16)
        q = jnp.dot(cq, wq_ref[...], preferred_element_type=F32)
        q_tab = jnp.where(is_nope, Q_SCALE, Q_SCALE * tab)
        for hh in range(MLA_HEADS):
            sl = slice(hh * HEAD_LANES, (hh + 1) * HEAD_LANES)
            q_ref[0, r0:r1, sl] = (q[:, sl] * q_tab).astype(BF16)

        ckv = (_rms(proj[:, c0:c0 + KV_LORA_RANK]) * kvg_ref[...]).astype(BF16)
        knope = jnp.dot(ckv, wk_ref[...], preferred_element_type=F32)
        r = proj[:, c1:c1 + LANES] * tab
        kf = r + pltpu.roll(r, QK_ROPE_DIM, axis=1)
        kf = jnp.where(is_nope, 0.0, kf)
        for hh in range(MLA_HEADS):
            sl = slice(hh * HEAD_LANES, (hh + 1) * HEAD_LANES)
            k_ref[0, r0:r1, sl] = (knope[:, sl] + kf).astype(BF16)
        vt_ref[0, :, r0:r1] = lax.dot_general(wvt_ref[...], ckv, (((1,), (1,)), ((), ())),
                                              preferred_element_type=F32).astype(BF16)

        gate_b = proj[:, c2:c2 + CONV_WIDTH]
        u = proj[:, c2 + CONV_WIDTH:c2 + 2 * CONV_WIDTH] * proj[:, c2 + 2 * CONV_WIDTH:c2 + 3 * CONV_WIDTH]
        cat = jnp.concatenate([prev, u], axis=0)
        u1 = pltpu.roll(cat, 1, axis=0)[8:]
        u2 = pltpu.roll(cat, 2, axis=0)[8:]
        prev = u[r1 - r0 - 8:]
        if r1 == t:
            carry_ref[...] = prev
        y = cb_ref[...] + cw_ref[0:1, :] * u2 + cw_ref[1:2, :] * u1 + cw_ref[2:3, :] * u
        y = gate_b * y
        for cc in range(CONV_WIDTH // LANES):
            sl = slice(cc * LANES, (cc + 1) * LANES)
            yv = y[:, sl]
            sq = yv * yv
            ms_lo = jnp.sum(jnp.where(lo, sq, 0.0), axis=-1, keepdims=True)
            ms_hi = jnp.sum(jnp.where(lo, 0.0, sq), axis=-1, keepdims=True)
            ms = jnp.where(lo, ms_lo, ms_hi) * (1.0 / (CONV_WIDTH // CONV_GROUPS))
            yc_ref[0, r0:r1, sl] = (yv * lax.rsqrt(ms + RMS_EPS) * og_ref[:, sl]).astype(BF16)


def _mixer_proj(x, mod, tab, win, qg, wq, kvg, wk, wvt, cw, cb, og):
    b, s, d = x.shape
    t = TOK_TILE
    nt = s // t
    const = lambda shape: pl.BlockSpec(shape, lambda bi, si: (0,) * len(shape))
    return pl.pallas_call(
        _proj_kernel,
        grid=(b, nt),
        in_specs=[pl.BlockSpec((1, t, d), lambda bi, si: (bi, si, 0)),
                  pl.BlockSpec((1, 6, d), lambda bi, si: (bi, 0, 0)),
                  pl.BlockSpec((1, t, LANES), lambda bi, si: (bi, si, 0)),
                  const(win.shape), const(qg.shape), const(wq.shape), const(kvg.shape),
                  const(wk.shape), const(wvt.shape), const(cw.shape), const(cb.shape), const(og.shape)],
        out_specs=[pl.BlockSpec((1, t, MLA_HEADS * HEAD_LANES), lambda bi, si: (bi, si, 0)),
                   pl.BlockSpec((1, t, MLA_HEADS * HEAD_LANES), lambda bi, si: (bi, si, 0)),
                   pl.BlockSpec((1, ATTN_WIDTH, t), lambda bi, si: (bi, 0, si)),
                   pl.BlockSpec((1, t, CONV_WIDTH), lambda bi, si: (bi, si, 0))],
        out_shape=[jax.ShapeDtypeStruct((b, s, MLA_HEADS * HEAD_LANES), BF16),
                   jax.ShapeDtypeStruct((b, s, MLA_HEADS * HEAD_LANES), BF16),
                   jax.ShapeDtypeStruct((b, ATTN_WIDTH, s), BF16),
                   jax.ShapeDtypeStruct((b, s, CONV_WIDTH), BF16)],
        scratch_shapes=[pltpu.VMEM((8, CONV_WIDTH), F32)],
        compiler_params=pltpu.CompilerParams(dimension_semantics=("arbitrary", "arbitrary"),
                                             vmem_limit_bytes=56 * 1024 * 1024, flags=SCHED_FLAGS),
        name="mixer_proj",
    )(x, mod, tab, win, qg, wq, kvg, wk, wvt, cw, cb, og)


def _attn_kernel(q_ref, k_ref, vt_ref, g_ref, o_ref, s_ref, ot_ref):
    seq = q_ref.shape[1]
    qry_chunk = lax.broadcasted_iota(jnp.int32, (1, Q_TILE), 1) // CHUNK
    ones = jnp.ones((16, K_TILE), BF16)
    dn = (((1,), (1,)), ((), ()))

    def scores(qi, hh):
        q0 = qi * Q_TILE
        hl = slice(hh * HEAD_LANES, (hh + 1) * HEAD_LANES)
        q = q_ref[0, q0:q0 + Q_TILE, hl]
        if qi > 0:
            s_ref[hh, 0:q0, :] = lax.dot_general(k_ref[0, 0:q0, hl], q, dn, preferred_element_type=F32)
        sd = lax.dot_general(k_ref[0, q0:q0 + Q_TILE, hl], q, dn, preferred_element_type=F32)
        for kc in range(Q_TILE // CHUNK):
            rows = slice(kc * CHUNK, (kc + 1) * CHUNK)
            s_ref[hh, q0 + kc * CHUNK:q0 + (kc + 1) * CHUNK, :] = jnp.where(qry_chunk >= kc, sd[rows], NEG)

    def softmax_pv(qi, hh):
        kmax = (qi + 1) * Q_TILE
        m = None
        for j in range(kmax // K_TILE):
            mj = jnp.max(s_ref[hh, j * K_TILE:(j + 1) * K_TILE, :], axis=0, keepdims=True)
            m = mj if m is None else jnp.maximum(m, mj)
        o = None
        for j in range(kmax // K_TILE):
            rows = slice(j * K_TILE, (j + 1) * K_TILE)
            p = jnp.exp2(s_ref[hh, rows, :] - m).astype(BF16)
            lhs = jnp.concatenate([vt_ref[0, hh * V_HEAD_DIM:(hh + 1) * V_HEAD_DIM, rows], ones], axis=0)
            oj = jnp.dot(lhs, p, preferred_element_type=F32)
            o = oj if o is None else o + oj
        o = o[0:V_HEAD_DIM] / o[V_HEAD_DIM:V_HEAD_DIM + 1]
        ms = jnp.mean(o * o, axis=0, keepdims=True)
        ot_ref[hh * V_HEAD_DIM:(hh + 1) * V_HEAD_DIM, :] = o * lax.rsqrt(ms + RMS_EPS)

    chains = [(qi, hh) for qi in range(seq // Q_TILE) for hh in range(2)]
    scores(*chains[0])
    for n, (qi, hh) in enumerate(chains):
        if n + 1 < len(chains):
            scores(*chains[n + 1])
        softmax_pv(qi, hh)
        if hh == 1:
            o_ref[0, qi * Q_TILE:(qi + 1) * Q_TILE, :] = (ot_ref[...].T * g_ref[...]).astype(BF16)


def _attention(q, k, vt, g):
    b, s, _ = q.shape
    pairs = MLA_HEADS // 2
    return pl.pallas_call(
        _attn_kernel,
        grid=(b, pairs),
        in_specs=[pl.BlockSpec((1, s, 2 * HEAD_LANES), lambda bi, pi: (bi, 0, pi)),
                  pl.BlockSpec((1, s, 2 * HEAD_LANES), lambda bi, pi: (bi, 0, pi)),
                  pl.BlockSpec((1, 2 * V_HEAD_DIM, s), lambda bi, pi: (bi, pi, 0)),
                  pl.BlockSpec((1, 2 * V_HEAD_DIM), lambda bi, pi: (0, pi))],
        out_specs=pl.BlockSpec((1, s, 2 * V_HEAD_DIM), lambda bi, pi: (bi, 0, pi)),
        out_shape=jax.ShapeDtypeStruct((b, s, ATTN_WIDTH), BF16),
        scratch_shapes=[pltpu.VMEM((2, s, Q_TILE), F32),
                        pltpu.VMEM((2 * V_HEAD_DIM, Q_TILE), F32)],
        compiler_params=pltpu.CompilerParams(dimension_semantics=("arbitrary",) * 2,
                                             vmem_limit_bytes=48 * 1024 * 1024, flags=SCHED_FLAGS),
        name="attention",
    )(q, k, vt, g)


def _ffn_kernel(x_ref, ya_ref, yc_ref, mod_ref, woa_ref, woc_ref, ln1g_ref, ln1b_ref, wup_ref,
                fcw_ref, fcb_ref, wdn_ref, ln2g_ref, ln2b_ref, o_ref, carry_ref, act_ref):
    t = TOK_TILE
    si = pl.program_id(1)
    gate_m = mod_ref[0, 2:3, :]
    shift_f = mod_ref[0, 3:4, :]
    scale_f = mod_ref[0, 4:5, :]
    gate_f = mod_ref[0, 5:6, :]

    @pl.when(si == 0)
    def _():
        carry_ref[...] = jnp.zeros(carry_ref.shape, F32)

    blocks = [(r, r + t // FFN_ROW_BLOCKS) for r in range(0, t, t // FFN_ROW_BLOCKS)]
    x1s, hs = [], []
    for r0, r1 in blocks:
        mix = (jnp.dot(ya_ref[0, r0:r1, :], woa_ref[...], preferred_element_type=F32)
               + jnp.dot(yc_ref[0, r0:r1, :], woc_ref[...], preferred_element_type=F32))
        x1 = _layer_norm(DEEPNORM_ALPHA * x_ref[0, r0:r1, :] + gate_m * mix) * ln1g_ref[...] + ln1b_ref[...]
        x1s.append(x1)
        hs.append((_layer_norm(x1) * (1.0 + scale_f) + shift_f).astype(BF16))

    def conv_cols(cols, h, prev, last):
        u = jnp.dot(h, wup_ref[:, cols], preferred_element_type=F32)
        cat = jnp.concatenate([prev, u], axis=0)
        u1 = pltpu.roll(cat, 1, axis=0)[8:]
        u2 = pltpu.roll(cat, 2, axis=0)[8:]
        tail = u[u.shape[0] - 8:]
        if last:
            carry_ref[:, cols] = tail
        y = fcb_ref[:, cols] + fcw_ref[0:1, cols] * u2 + fcw_ref[1:2, cols] * u1 + fcw_ref[2:3, cols] * u
        return y, tail

    for c in range(N_FF_CHUNKS):
        gcols = slice(c * FF_CHUNK, (c + 1) * FF_CHUNK)
        vcols = slice(D_FF + c * FF_CHUNK, D_FF + (c + 1) * FF_CHUNK)
        gprev, vprev = carry_ref[:, gcols], carry_ref[:, vcols]
        for bi, ((r0, r1), h) in enumerate(zip(blocks, hs)):
            last = bi == len(blocks) - 1
            g, gprev = conv_cols(gcols, h, gprev, last)
            v, vprev = conv_cols(vcols, h, vprev, last)
            act_ref[r0:r1, c * FF_CHUNK:(c + 1) * FF_CHUNK] = (g * _sigmoid(g) * v).astype(BF16)

    for (r0, r1), x1 in zip(blocks, x1s):
        ff = jnp.dot(act_ref[r0:r1, :], wdn_ref[...], preferred_element_type=F32)
        o_ref[0, r0:r1, :] = (_layer_norm(DEEPNORM_ALPHA * x1 + gate_f * ff) * ln2g_ref[...]
                              + ln2b_ref[...])


def _out_ffn(x, ya, yc, mod, woa, woc, ln1g, ln1b, wup, fcw, fcb, wdn, ln2g, ln2b):
    b, s, d = x.shape
    t = TOK_TILE
    const = lambda shape: pl.BlockSpec(shape, lambda bi, si: (0,) * len(shape),
                                       pipeline_mode=pl.Buffered(1))
    tok = lambda width: pl.BlockSpec((1, t, width), lambda bi, si: (bi, si, 0))
    return pl.pallas_call(
        _ffn_kernel,
        grid=(b, s // t),
        in_specs=[tok(d), tok(ATTN_WIDTH), tok(CONV_WIDTH),
                  pl.BlockSpec((1, 6, d), lambda bi, si: (bi, 0, 0)),
                  const(woa.shape), const(woc.shape), const(ln1g.shape), const(ln1b.shape),
                  const(wup.shape), const(fcw.shape), const(fcb.shape), const(wdn.shape),
                  const(ln2g.shape), const(ln2b.shape)],
        out_specs=tok(d),
        out_shape=jax.ShapeDtypeStruct((b, s, d), F32),
        scratch_shapes=[pltpu.VMEM((8, 2 * D_FF), F32),
                        pltpu.VMEM((t, D_FF), BF16)],
        compiler_params=pltpu.CompilerParams(dimension_semantics=("arbitrary", "arbitrary"),
                                             vmem_limit_bytes=56 * 1024 * 1024, flags=SCHED_FLAGS),
        name="out_ffn",
    )(x, ya, yc, mod, woa, woc, ln1g, ln1b, wup, fcw, fcb, wdn, ln2g, ln2b)


def _swap_halves(w):
    half = w.shape[-1] // 2
    return jnp.concatenate([w[..., half:], w[..., :half]], axis=-1)


def _prep_in_proj(w_in):
    c0 = Q_LORA_RANK + KV_LORA_RANK
    kr = w_in[:, c0:c0 + QK_ROPE_DIM]
    krs = _swap_halves(kr)
    return jnp.concatenate([w_in[:, :c0], kr, krs, kr, krs, w_in[:, c0 + QK_ROPE_DIM:]], axis=1).astype(BF16)


def _prep_q_up(w_q_up):
    w = w_q_up.reshape(Q_LORA_RANK, MLA_HEADS, QK_NOPE_DIM + QK_ROPE_DIM)
    rope = w[..., QK_NOPE_DIM:]
    w = jnp.concatenate([w[..., :QK_NOPE_DIM], rope, _swap_halves(rope)], axis=-1)
    return w.reshape(Q_LORA_RANK, MLA_HEADS * HEAD_LANES).astype(BF16)


def _prep_kv_up(w_kv_up):
    w = w_kv_up.reshape(KV_LORA_RANK, MLA_HEADS, QK_NOPE_DIM + V_HEAD_DIM)
    knope = w[..., :QK_NOPE_DIM]
    wk = jnp.concatenate([knope, jnp.zeros_like(knope)], axis=-1).reshape(KV_LORA_RANK, MLA_HEADS * HEAD_LANES)
    wvt = w[..., QK_NOPE_DIM:].reshape(KV_LORA_RANK, ATTN_WIDTH).T
    return wk.astype(BF16), wvt.astype(BF16)


def kernel(x, c, positions, w_ada, b_ada, w_in, q_norm_g, w_q_up, kv_norm_g, w_kv_up, conv_w, conv_b,
           out_norm_g, w_out, ln1_g, ln1_b, w_up, ffn_conv_w, ffn_conv_b, w_down, ln2_g, ln2_b):
    b, s, d = x.shape
    depth = w_ada.shape[0]
    tab = _rope_table(positions)
    for l in range(depth):
        mod = _adaln_mod(c, w_ada[l], b_ada[l]).reshape(b, 6, d)
        wk, wvt = _prep_kv_up(w_kv_up[l])
        q, k, vt, yc = _mixer_proj(
            x, mod, tab, _prep_in_proj(w_in[l]), q_norm_g[l].reshape(1, -1), _prep_q_up(w_q_up[l]),
            kv_norm_g[l].reshape(1, -1), wk, wvt, conv_w[l], conv_b[l].reshape(1, -1),
            out_norm_g[l, ATTN_WIDTH:].reshape(1, -1))
        ya = _attention(q, k, vt, out_norm_g[l, :ATTN_WIDTH].reshape(1, -1))
        x = _out_ffn(
            x, ya, yc, mod, w_out[l, :ATTN_WIDTH].astype(BF16), w_out[l, ATTN_WIDTH:].astype(BF16),
            ln1_g[l].reshape(1, -1), ln1_b[l].reshape(1, -1), w_up[l].astype(BF16),
            ffn_conv_w[l], ffn_conv_b[l].reshape(1, -1),
            w_down[l].astype(BF16), ln2_g[l].reshape(1, -1), ln2_b[l].reshape(1, -1))
    return x
```

```python
import jax
import jax.numpy as jnp
from jax import lax
from jax.experimental import pallas as pl
from jax.experimental.pallas import tpu as pltpu

D_MODEL = 1024
SEQ = 2048
CHUNK = 64
MLA_HEADS = 8
QK_NOPE_DIM = 64
QK_ROPE_DIM = 32
V_HEAD_DIM = 64
Q_LORA_RANK = 256
KV_LORA_RANK = 128
ATTN_WIDTH = MLA_HEADS * V_HEAD_DIM
CONV_WIDTH = D_MODEL - ATTN_WIDTH
CONV_GROUPS = 8
D_FF = 2816
ROPE_THETA = 10000.0
RMS_EPS = 1e-6
LN_EPS = 1e-5
DEEPNORM_ALPHA = 2.0 ** 0.25

LANES = 128
HEAD_LANES = 128
IN_EXT = Q_LORA_RANK + KV_LORA_RANK + LANES + 3 * CONV_WIDTH
TOK_TILE = 512
PROJ_TILE = 1024
Q_TILE = 512
K_TILE = 256
FF_CHUNK = 256
N_FF_CHUNKS = D_FF // FF_CHUNK
FFN_ROW_BLOCKS = 2
PROJ_ROW_BLOCKS = 4
NEG = -1e30
Q_SCALE = (QK_NOPE_DIM + QK_ROPE_DIM) ** -0.5 * 1.4426950408889634

F32 = jnp.float32
BF16 = jnp.bfloat16
SCHED_FLAGS = None


def _layer_norm(x):
    mu = jnp.mean(x, axis=-1, keepdims=True)
    xc = x - mu
    var = jnp.mean(xc * xc, axis=-1, keepdims=True)
    return xc * lax.rsqrt(var + LN_EPS)


def _rms(x):
    return x * lax.rsqrt(jnp.mean(x * x, axis=-1, keepdims=True) + RMS_EPS)


def _sigmoid(x):
    return 1.0 / (1.0 + jnp.exp(-x))


def _split3(x):
    hi = x.astype(BF16)
    r1 = x - hi.astype(F32)
    mid = r1.astype(BF16)
    lo = (r1 - mid.astype(F32)).astype(BF16)
    return hi, mid, lo


def _rope_kernel(pos_ref, freq_ref, e_ref, tab_ref):
    ang = freq_ref[...] * pos_ref[0].astype(F32)
    pieces = jnp.concatenate(_split3(jnp.cos(ang)) + _split3(jnp.sin(ang)), axis=0)
    tab_ref[0] = lax.dot_general(pieces, e_ref[...], (((0,), (0,)), ((), ())),
                                 preferred_element_type=F32)


def _rope_expansion():
    half = QK_ROPE_DIM // 2
    src = jnp.arange(6 * half)[:, None]
    dst = jnp.arange(LANES)[None, :]
    same_freq = src % half == dst % half
    is_sin_row = src >= 3 * half
    group = dst // half % 4
    e_cos = jnp.where(same_freq & ~is_sin_row & (group < 2), 1.0, 0.0)
    e_sin = jnp.where(same_freq & is_sin_row & (group == 2), -1.0,
                      jnp.where(same_freq & is_sin_row & (group == 3), 1.0, 0.0))
    return (e_cos + e_sin).astype(BF16)


def _rope_table(positions):
    b, s = positions.shape
    half = QK_ROPE_DIM // 2
    inv_freq = ROPE_THETA ** (-jnp.arange(0, QK_ROPE_DIM, 2, dtype=F32) / QK_ROPE_DIM)
    e = _rope_expansion()
    return pl.pallas_call(
        _rope_kernel,
        grid=(b,),
        in_specs=[pl.BlockSpec((1, 1, s), lambda i: (i, 0, 0)),
                  pl.BlockSpec((half, 1), lambda i: (0, 0)),
                  pl.BlockSpec(e.shape, lambda i: (0, 0))],
        out_specs=pl.BlockSpec((1, s, LANES), lambda i: (i, 0, 0)),
        out_shape=jax.ShapeDtypeStruct((b, s, LANES), F32),
        compiler_params=pltpu.CompilerParams(dimension_semantics=("arbitrary",),
                                             vmem_limit_bytes=32 * 1024 * 1024),
        name="rope_table",
    )(positions.reshape(b, 1, s), inv_freq.reshape(half, 1), e)


def _mod_kernel(c_ref, w_ref, b_ref, o_ref):
    c = c_ref[...]
    act = c * _sigmoid(c)
    o_ref[...] = jnp.dot(act, w_ref[...], preferred_element_type=F32,
                         precision=lax.Precision.HIGHEST) + b_ref[...]


def _adaln_mod(c, w_ada, b_ada):
    b, d = c.shape
    n = w_ada.shape[1]
    return pl.pallas_call(
        _mod_kernel,
        grid=(n // d,),
        in_specs=[pl.BlockSpec((b, d), lambda j: (0, 0)),
                  pl.BlockSpec((d, d), lambda j: (0, j)),
                  pl.BlockSpec((1, d), lambda j: (0, j))],
        out_specs=pl.BlockSpec((b, d), lambda j: (0, j)),
        out_shape=jax.ShapeDtypeStruct((b, n), F32),
        compiler_params=pltpu.CompilerParams(dimension_semantics=("arbitrary",),
                                             vmem_limit_bytes=32 * 1024 * 1024),
        name="adaln_mod",
    )(c, w_ada, b_ada.reshape(1, n))


def _proj_kernel(x_ref, mod_ref, tab_ref, win_ref, qg_ref, wq_ref, kvg_ref, wk_ref, wvt_ref,
                 cw_ref, cb_ref, og_ref, q_ref, k_ref, vt_ref, yc_ref, carry_ref):
    t = PROJ_TILE
    si = pl.program_id(1)
    shift = mod_ref[0, 0:1, :]
    scale = mod_ref[0, 1:2, :]
    lane = lax.broadcasted_iota(jnp.int32, (1, LANES), 1)
    is_nope = lane < QK_NOPE_DIM
    lo = lane < (CONV_WIDTH // CONV_GROUPS)
    c0 = Q_LORA_RANK
    c1 = c0 + KV_LORA_RANK
    c2 = c1 + LANES

    @pl.when(si == 0)
    def _():
        carry_ref[...] = jnp.zeros((8, CONV_WIDTH), F32)

    prev = carry_ref[...]
    for r0 in range(0, t, t // PROJ_ROW_BLOCKS):
        r1 = r0 + t // PROJ_ROW_BLOCKS
        h = (_layer_norm(x_ref[0, r0:r1, :]) * (1.0 + scale) + shift).astype(BF16)
        proj = jnp.dot(h, win_ref[...], preferred_element_type=F32)
        tab = tab_ref[0, r0:r1, :]

        cq = (_rms(proj[:, 0:Q_LORA_RANK]) * qg_ref[...]).astype(BF16)
        q = jnp.dot(cq, wq_ref[...], preferred_element_type=F32)
        q_tab = jnp.where(is_nope, Q_SCALE, Q_SCALE * tab)
        for hh in range(MLA_HEADS):
            sl = slice(hh * HEAD_LANES, (hh + 1) * HEAD_LANES)
            q_ref[0, r0:r1, sl] = (q[:, sl] * q_tab).astype(BF16)

        ckv = (_rms(proj[:, c0:c0 + KV_LORA_RANK]) * kvg_ref[...]).astype(BF16)
        knope = jnp.dot(ckv, wk_ref[...], preferred_element_type=F32)
        r = proj[:, c1:c1 + LANES] * tab
        kf = r + pltpu.roll(r, QK_ROPE_DIM, axis=1)
        kf = jnp.where(is_nope, 0.0, kf)
        for hh in range(MLA_HEADS):
            sl = slice(hh * HEAD_LANES, (hh + 1) * HEAD_LANES)
            k_ref[0, r0:r1, sl] = (knope[:, sl] + kf).astype(BF16)
        vt_ref[0, :, r0:r1] = lax.dot_general(wvt_ref[...], ckv, (((1,), (1,)), ((), ())),
                                              preferred_element_type=F32).astype(BF16)

        gate_b = proj[:, c2:c2 + CONV_WIDTH]
        u = proj[:, c2 + CONV_WIDTH:c2 + 2 * CONV_WIDTH] * proj[:, c2 + 2 * CONV_WIDTH:c2 + 3 * CONV_WIDTH]
        cat = jnp.concatenate([prev, u], axis=0)
        u1 = pltpu.roll(cat, 1, axis=0)[8:]
        u2 = pltpu.roll(cat, 2, axis=0)[8:]
        prev = u[r1 - r0 - 8:]
        if r1 == t:
            carry_ref[...] = prev
        y = cb_ref[...] + cw_ref[0:1, :] * u2 + cw_ref[1:2, :] * u1 + cw_ref[2:3, :] * u
        y = gate_b * y
        for cc in range(CONV_WIDTH // LANES):
            sl = slice(cc * LANES, (cc + 1) * LANES)
            yv = y[:, sl]
            sq = yv * yv
            ms_lo = jnp.sum(jnp.where(lo, sq, 0.0), axis=-1, keepdims=True)
            ms_hi = jnp.sum(jnp.where(lo, 0.0, sq), axis=-1, keepdims=True)
            ms = jnp.where(lo, ms_lo, ms_hi) * (1.0 / (CONV_WIDTH // CONV_GROUPS))
            yc_ref[0, r0:r1, sl] = (yv * lax.rsqrt(ms + RMS_EPS) * og_ref[:, sl]).astype(BF16)


def _mixer_proj(x, mod, tab, win, qg, wq, kvg, wk, wvt, cw, cb, og):
    b, s, d = x.shape
    t = PROJ_TILE
    nt = s // t
    const = lambda shape: pl.BlockSpec(shape, lambda bi, si: (0,) * len(shape))
    return pl.pallas_call(
        _proj_kernel,
        grid=(b, nt),
        in_specs=[pl.BlockSpec((1, t, d), lambda bi, si: (bi, si, 0)),
                  pl.BlockSpec((1, 6, d), lambda bi, si: (bi, 0, 0)),
                  pl.BlockSpec((1, t, LANES), lambda bi, si: (bi, si, 0)),
                  const(win.shape), const(qg.shape), const(wq.shape), const(kvg.shape),
                  const(wk.shape), const(wvt.shape), const(cw.shape), const(cb.shape), const(og.shape)],
        out_specs=[pl.BlockSpec((1, t, MLA_HEADS * HEAD_LANES), lambda bi, si: (bi, si, 0)),
                   pl.BlockSpec((1, t, MLA_HEADS * HEAD_LANES), lambda bi, si: (bi, si, 0)),
                   pl.BlockSpec((1, ATTN_WIDTH, t), lambda bi, si: (bi, 0, si)),
                   pl.BlockSpec((1, t, CONV_WIDTH), lambda bi, si: (bi, si, 0))],
        out_shape=[jax.ShapeDtypeStruct((b, s, MLA_HEADS * HEAD_LANES), BF16),
                   jax.ShapeDtypeStruct((b, s, MLA_HEADS * HEAD_LANES), BF16),
                   jax.ShapeDtypeStruct((b, ATTN_WIDTH, s), BF16),
                   jax.ShapeDtypeStruct((b, s, CONV_WIDTH), BF16)],
        scratch_shapes=[pltpu.VMEM((8, CONV_WIDTH), F32)],
        compiler_params=pltpu.CompilerParams(dimension_semantics=("arbitrary", "arbitrary"),
                                             vmem_limit_bytes=56 * 1024 * 1024, flags=SCHED_FLAGS),
        name="mixer_proj",
    )(x, mod, tab, win, qg, wq, kvg, wk, wvt, cw, cb, og)


def _attn_kernel(q_ref, k_ref, vt_ref, g_ref, o_ref, s_ref, ot_ref):
    seq = q_ref.shape[1]
    qry_chunk = lax.broadcasted_iota(jnp.int32, (1, Q_TILE), 1) // CHUNK
    ones = jnp.ones((16, K_TILE), BF16)
    dn = (((1,), (1,)), ((), ()))

    def scores(qi, hh):
        q0 = qi * Q_TILE
        hl = slice(hh * HEAD_LANES, (hh + 1) * HEAD_LANES)
        q = q_ref[0, q0:q0 + Q_TILE, hl]
        if qi > 0:
            s_ref[hh, 0:q0, :] = lax.dot_general(k_ref[0, 0:q0, hl], q, dn, preferred_element_type=F32)
        sd = lax.dot_general(k_ref[0, q0:q0 + Q_TILE, hl], q, dn, preferred_element_type=F32)
        for kc in range(Q_TILE // CHUNK):
            rows = slice(kc * CHUNK, (kc + 1) * CHUNK)
            s_ref[hh, q0 + kc * CHUNK:q0 + (kc + 1) * CHUNK, :] = jnp.where(qry_chunk >= kc, sd[rows], NEG)

    def softmax_pv(qi, hh):
        kmax = (qi + 1) * Q_TILE
        m = None
        for j in range(kmax // K_TILE):
            mj = jnp.max(s_ref[hh, j * K_TILE:(j + 1) * K_TILE, :], axis=0, keepdims=True)
            m = mj if m is None else jnp.maximum(m, mj)
        o = None
        for j in range(kmax // K_TILE):
            rows = slice(j * K_TILE, (j + 1) * K_TILE)
            p = jnp.exp2(s_ref[hh, rows, :] - m).astype(BF16)
            lhs = jnp.concatenate([vt_ref[0, hh * V_HEAD_DIM:(hh + 1) * V_HEAD_DIM, rows], ones], axis=0)
            oj = jnp.dot(lhs, p, preferred_element_type=F32)
            o = oj if o is None else o + oj
        o = o[0:V_HEAD_DIM] / o[V_HEAD_DIM:V_HEAD_DIM + 1]
        ms = jnp.mean(o * o, axis=0, keepdims=True)
        ot_ref[hh * V_HEAD_DIM:(hh + 1) * V_HEAD_DIM, :] = o * lax.rsqrt(ms + RMS_EPS)

    chains = [(qi, hh) for qi in reversed(range(seq // Q_TILE)) for hh in range(2)]
    scores(*chains[0])
    for n, (qi, hh) in enumerate(chains):
        if n + 1 < len(chains):
            scores(*chains[n + 1])
        softmax_pv(qi, hh)
        if hh == 1:
            o_ref[0, qi * Q_TILE:(qi + 1) * Q_TILE, :] = (ot_ref[...].T * g_ref[...]).astype(BF16)


def _attention(q, k, vt, g):
    b, s, _ = q.shape
    pairs = MLA_HEADS // 2
    return pl.pallas_call(
        _attn_kernel,
        grid=(b, pairs),
        in_specs=[pl.BlockSpec((1, s, 2 * HEAD_LANES), lambda bi, pi: (bi, 0, pi)),
                  pl.BlockSpec((1, s, 2 * HEAD_LANES), lambda bi, pi: (bi, 0, pi)),
                  pl.BlockSpec((1, 2 * V_HEAD_DIM, s), lambda bi, pi: (bi, pi, 0)),
                  pl.BlockSpec((1, 2 * V_HEAD_DIM), lambda bi, pi: (0, pi))],
        out_specs=pl.BlockSpec((1, s, 2 * V_HEAD_DIM), lambda bi, pi: (bi, 0, pi)),
        out_shape=jax.ShapeDtypeStruct((b, s, ATTN_WIDTH), BF16),
        scratch_shapes=[pltpu.VMEM((2, s, Q_TILE), F32),
                        pltpu.VMEM((2 * V_HEAD_DIM, Q_TILE), F32)],
        compiler_params=pltpu.CompilerParams(dimension_semantics=("arbitrary",) * 2,
                                             vmem_limit_bytes=48 * 1024 * 1024, flags=SCHED_FLAGS),
        name="attention",
    )(q, k, vt, g)


def _ffn_kernel(x_ref, ya_ref, yc_ref, mod_ref, woa_ref, woc_ref, ln1g_ref, ln1b_ref, wup_ref,
                fcw_ref, fcb_ref, wdn_ref, ln2g_ref, ln2b_ref, o_ref, carry_ref, act_ref):
    t = TOK_TILE
    si = pl.program_id(1)
    gate_m = mod_ref[0, 2:3, :]
    shift_f = mod_ref[0, 3:4, :]
    scale_f = mod_ref[0, 4:5, :]
    gate_f = mod_ref[0, 5:6, :]

    @pl.when(si == 0)
    def _():
        carry_ref[...] = jnp.zeros(carry_ref.shape, F32)

    blocks = [(r, r + t // FFN_ROW_BLOCKS) for r in range(0, t, t // FFN_ROW_BLOCKS)]
    x1s, hs = [], []
    for r0, r1 in blocks:
        mix = (jnp.dot(ya_ref[0, r0:r1, :], woa_ref[...], preferred_element_type=F32)
               + jnp.dot(yc_ref[0, r0:r1, :], woc_ref[...], preferred_element_type=F32))
        x1 = _layer_norm(DEEPNORM_ALPHA * x_ref[0, r0:r1, :] + gate_m * mix) * ln1g_ref[...] + ln1b_ref[...]
        x1s.append(x1)
        hs.append((_layer_norm(x1) * (1.0 + scale_f) + shift_f).astype(BF16))

    def conv_cols(cols, h, prev, last):
        u = jnp.dot(h, wup_ref[:, cols], preferred_element_type=F32)
        cat = jnp.concatenate([prev, u], axis=0)
        u1 = pltpu.roll(cat, 1, axis=0)[8:]
        u2 = pltpu.roll(cat, 2, axis=0)[8:]
        tail = u[u.shape[0] - 8:]
        if last:
            carry_ref[:, cols] = tail
        y = fcb_ref[:, cols] + fcw_ref[0:1, cols] * u2 + fcw_ref[1:2, cols] * u1 + fcw_ref[2:3, cols] * u
        return y, tail

    for c in range(N_FF_CHUNKS):
        gcols = slice(c * FF_CHUNK, (c + 1) * FF_CHUNK)
        vcols = slice(D_FF + c * FF_CHUNK, D_FF + (c + 1) * FF_CHUNK)
        gprev, vprev = carry_ref[:, gcols], carry_ref[:, vcols]
        for bi, ((r0, r1), h) in enumerate(zip(blocks, hs)):
            last = bi == len(blocks) - 1
            g, gprev = conv_cols(gcols, h, gprev, last)
            v, vprev = conv_cols(vcols, h, vprev, last)
            act_ref[r0:r1, c * FF_CHUNK:(c + 1) * FF_CHUNK] = (g * _sigmoid(g) * v).astype(BF16)

    for (r0, r1), x1 in zip(blocks, x1s):
        ff = jnp.dot(act_ref[r0:r1, :], wdn_ref[...], preferred_element_type=F32)
        o_ref[0, r0:r1, :] = (_layer_norm(DEEPNORM_ALPHA * x1 + gate_f * ff) * ln2g_ref[...]
                              + ln2b_ref[...])


def _out_ffn(x, ya, yc, mod, woa, woc, ln1g, ln1b, wup, fcw, fcb, wdn, ln2g, ln2b):
    b, s, d = x.shape
    t = TOK_TILE
    const = lambda shape: pl.BlockSpec(shape, lambda bi, si: (0,) * len(shape),
                                       pipeline_mode=pl.Buffered(1))
    tok = lambda width: pl.BlockSpec((1, t, width), lambda bi, si: (bi, si, 0))
    return pl.pallas_call(
        _ffn_kernel,
        grid=(b, s // t),
        in_specs=[tok(d), tok(ATTN_WIDTH), tok(CONV_WIDTH),
                  pl.BlockSpec((1, 6, d), lambda bi, si: (bi, 0, 0)),
                  const(woa.shape), const(woc.shape), const(ln1g.shape), const(ln1b.shape),
                  const(wup.shape), const(fcw.shape), const(fcb.shape), const(wdn.shape),
                  const(ln2g.shape), const(ln2b.shape)],
        out_specs=tok(d),
        out_shape=jax.ShapeDtypeStruct((b, s, d), F32),
        scratch_shapes=[pltpu.VMEM((8, 2 * D_FF), F32),
                        pltpu.VMEM((t, D_FF), BF16)],
        compiler_params=pltpu.CompilerParams(dimension_semantics=("arbitrary", "arbitrary"),
                                             vmem_limit_bytes=56 * 1024 * 1024, flags=SCHED_FLAGS),
        name="out_ffn",
    )(x, ya, yc, mod, woa, woc, ln1g, ln1b, wup, fcw, fcb, wdn, ln2g, ln2b)


def _swap_halves(w):
    half = w.shape[-1] // 2
    return jnp.concatenate([w[..., half:], w[..., :half]], axis=-1)


def _prep_in_proj(w_in):
    c0 = Q_LORA_RANK + KV_LORA_RANK
    kr = w_in[:, c0:c0 + QK_ROPE_DIM]
    krs = _swap_halves(kr)
    return jnp.concatenate([w_in[:, :c0], kr, krs, kr, krs, w_in[:, c0 + QK_ROPE_DIM:]], axis=1).astype(BF16)


def _prep_q_up(w_q_up):
    w = w_q_up.reshape(Q_LORA_RANK, MLA_HEADS, QK_NOPE_DIM + QK_ROPE_DIM)
    rope = w[..., QK_NOPE_DIM:]
    w = jnp.concatenate([w[..., :QK_NOPE_DIM], rope, _swap_halves(rope)], axis=-1)
    return w.reshape(Q_LORA_RANK, MLA_HEADS * HEAD_LANES).astype(BF16)


def _prep_kv_up(w_kv_up):
    w = w_kv_up.reshape(KV_LORA_RANK, MLA_HEADS, QK_NOPE_DIM + V_HEAD_DIM)
    knope = w[..., :QK_NOPE_DIM]
    wk = jnp.concatenate([knope, jnp.zeros_like(knope)], axis=-1).reshape(KV_LORA_RANK, MLA_HEADS * HEAD_LANES)
    wvt = w[..., QK_NOPE_DIM:].reshape(KV_LORA_RANK, ATTN_WIDTH).T
    return wk.astype(BF16), wvt.astype(BF16)


def kernel(x, c, positions, w_ada, b_ada, w_in, q_norm_g, w_q_up, kv_norm_g, w_kv_up, conv_w, conv_b,
           out_norm_g, w_out, ln1_g, ln1_b, w_up, ffn_conv_w, ffn_conv_b, w_down, ln2_g, ln2_b):
    b, s, d = x.shape
    depth = w_ada.shape[0]
    tab = _rope_table(positions)
    for l in range(depth):
        mod = _adaln_mod(c, w_ada[l], b_ada[l]).reshape(b, 6, d)
        wk, wvt = _prep_kv_up(w_kv_up[l])
        q, k, vt, yc = _mixer_proj(
            x, mod, tab, _prep_in_proj(w_in[l]), q_norm_g[l].reshape(1, -1), _prep_q_up(w_q_up[l]),
            kv_norm_g[l].reshape(1, -1), wk, wvt, conv_w[l], conv_b[l].reshape(1, -1),
            out_norm_g[l, ATTN_WIDTH:].reshape(1, -1))
        ya = _attention(q, k, vt, out_norm_g[l, :ATTN_WIDTH].reshape(1, -1))
        x = _out_ffn(
            x, ya, yc, mod, w_out[l, :ATTN_WIDTH].astype(BF16), w_out[l, ATTN_WIDTH:].astype(BF16),
            ln1_g[l].reshape(1, -1), ln1_b[l].reshape(1, -1), w_up[l].astype(BF16),
            ffn_conv_w[l], ffn_conv_b[l].reshape(1, -1),
            w_down[l].astype(BF16), ln2_g[l].reshape(1, -1), ln2_b[l].reshape(1, -1))
    return x
```

```python
import jax
import jax.numpy as jnp
from jax import lax
from jax.experimental import pallas as pl
from jax.experimental.pallas import tpu as pltpu

D_MODEL = 1024
SEQ = 2048
CHUNK = 64
MLA_HEADS = 8
QK_NOPE_DIM = 64
QK_ROPE_DIM = 32
V_HEAD_DIM = 64
Q_LORA_RANK = 256
KV_LORA_RANK = 128
ATTN_WIDTH = MLA_HEADS * V_HEAD_DIM
CONV_WIDTH = D_MODEL - ATTN_WIDTH
CONV_GROUPS = 8
D_FF = 2816
ROPE_THETA = 10000.0
RMS_EPS = 1e-6
LN_EPS = 1e-5
DEEPNORM_ALPHA = 2.0 ** 0.25

LANES = 128
HEAD_LANES = 128
IN_EXT = Q_LORA_RANK + KV_LORA_RANK + LANES + 3 * CONV_WIDTH
TOK_TILE = 512
PROJ_TILE = 1024
Q_TILE = 512
K_TILE = 256
HALF = Q_TILE // 2
assert K_TILE == HALF
FF_CHUNK = 256
N_FF_CHUNKS = D_FF // FF_CHUNK
FFN_ROW_BLOCKS = 2
PROJ_ROW_BLOCKS = 4
NEG = -1e30
Q_SCALE = (QK_NOPE_DIM + QK_ROPE_DIM) ** -0.5 * 1.4426950408889634

F32 = jnp.float32
BF16 = jnp.bfloat16
SCHED_FLAGS = None


def _layer_norm(x):
    mu = jnp.mean(x, axis=-1, keepdims=True)
    xc = x - mu
    var = jnp.mean(xc * xc, axis=-1, keepdims=True)
    return xc * lax.rsqrt(var + LN_EPS)


def _rms(x):
    return x * lax.rsqrt(jnp.mean(x * x, axis=-1, keepdims=True) + RMS_EPS)


def _sigmoid(x):
    return 1.0 / (1.0 + jnp.exp(-x))


def _split3(x):
    hi = x.astype(BF16)
    r1 = x - hi.astype(F32)
    mid = r1.astype(BF16)
    lo = (r1 - mid.astype(F32)).astype(BF16)
    return hi, mid, lo


def _rope_kernel(pos_ref, freq_ref, e_ref, tab_ref):
    ang = freq_ref[...] * pos_ref[0].astype(F32)
    pieces = jnp.concatenate(_split3(jnp.cos(ang)) + _split3(jnp.sin(ang)), axis=0)
    tab_ref[0] = lax.dot_general(pieces, e_ref[...], (((0,), (0,)), ((), ())),
                                 preferred_element_type=F32)


def _rope_expansion():
    half = QK_ROPE_DIM // 2
    src = jnp.arange(6 * half)[:, None]
    dst = jnp.arange(LANES)[None, :]
    same_freq = src % half == dst % half
    is_sin_row = src >= 3 * half
    group = dst // half % 4
    e_cos = jnp.where(same_freq & ~is_sin_row & (group < 2), 1.0, 0.0)
    e_sin = jnp.where(same_freq & is_sin_row & (group == 2), -1.0,
                      jnp.where(same_freq & is_sin_row & (group == 3), 1.0, 0.0))
    return (e_cos + e_sin).astype(BF16)


def _rope_table(positions):
    b, s = positions.shape
    half = QK_ROPE_DIM // 2
    inv_freq = ROPE_THETA ** (-jnp.arange(0, QK_ROPE_DIM, 2, dtype=F32) / QK_ROPE_DIM)
    e = _rope_expansion()
    return pl.pallas_call(
        _rope_kernel,
        grid=(b,),
        in_specs=[pl.BlockSpec((1, 1, s), lambda i: (i, 0, 0)),
                  pl.BlockSpec((half, 1), lambda i: (0, 0)),
                  pl.BlockSpec(e.shape, lambda i: (0, 0))],
        out_specs=pl.BlockSpec((1, s, LANES), lambda i: (i, 0, 0)),
        out_shape=jax.ShapeDtypeStruct((b, s, LANES), F32),
        compiler_params=pltpu.CompilerParams(dimension_semantics=("arbitrary",),
                                             vmem_limit_bytes=32 * 1024 * 1024),
        name="rope_table",
    )(positions.reshape(b, 1, s), inv_freq.reshape(half, 1), e)


def _mod_kernel(c_ref, w_ref, b_ref, o_ref):
    c = c_ref[...]
    act = c * _sigmoid(c)
    o_ref[...] = jnp.dot(act, w_ref[...], preferred_element_type=F32,
                         precision=lax.Precision.HIGHEST) + b_ref[...]


def _adaln_mod(c, w_ada, b_ada):
    b, d = c.shape
    n = w_ada.shape[1]
    return pl.pallas_call(
        _mod_kernel,
        grid=(n // d,),
        in_specs=[pl.BlockSpec((b, d), lambda j: (0, 0)),
                  pl.BlockSpec((d, d), lambda j: (0, j)),
                  pl.BlockSpec((1, d), lambda j: (0, j))],
        out_specs=pl.BlockSpec((b, d), lambda j: (0, j)),
        out_shape=jax.ShapeDtypeStruct((b, n), F32),
        compiler_params=pltpu.CompilerParams(dimension_semantics=("arbitrary",),
                                             vmem_limit_bytes=32 * 1024 * 1024),
        name="adaln_mod",
    )(c, w_ada, b_ada.reshape(1, n))


def _proj_kernel(x_ref, mod_ref, tab_ref, win_ref, qg_ref, wq_ref, kvg_ref, wk_ref, wvt_ref,
                 cw_ref, cb_ref, og_ref, q_ref, k_ref, vt_ref, yc_ref, carry_ref):
    t = PROJ_TILE
    si = pl.program_id(1)
    shift = mod_ref[0, 0:1, :]
    scale = mod_ref[0, 1:2, :]
    lane = lax.broadcasted_iota(jnp.int32, (1, LANES), 1)
    is_nope = lane < QK_NOPE_DIM
    lo = lane < (CONV_WIDTH // CONV_GROUPS)
    c0 = Q_LORA_RANK
    c1 = c0 + KV_LORA_RANK
    c2 = c1 + LANES

    @pl.when(si == 0)
    def _():
        carry_ref[...] = jnp.zeros((8, CONV_WIDTH), F32)

    prev = carry_ref[...]
    for r0 in range(0, t, t // PROJ_ROW_BLOCKS):
        r1 = r0 + t // PROJ_ROW_BLOCKS
        h = (_layer_norm(x_ref[0, r0:r1, :]) * (1.0 + scale) + shift).astype(BF16)
        proj = jnp.dot(h, win_ref[...], preferred_element_type=F32)
        tab = tab_ref[0, r0:r1, :]

        cq = (_rms(proj[:, 0:Q_LORA_RANK]) * qg_ref[...]).astype(BF16)
        q = jnp.dot(cq, wq_ref[...], preferred_element_type=F32)
        q_tab = jnp.where(is_nope, Q_SCALE, Q_SCALE * tab)
        for hh in range(MLA_HEADS):
            sl = slice(hh * HEAD_LANES, (hh + 1) * HEAD_LANES)
            q_ref[0, r0:r1, sl] = (q[:, sl] * q_tab).astype(BF16)

        ckv = (_rms(proj[:, c0:c0 + KV_LORA_RANK]) * kvg_ref[...]).astype(BF16)
        knope = jnp.dot(ckv, wk_ref[...], preferred_element_type=F32)
        r = proj[:, c1:c1 + LANES] * tab
        kf = r + pltpu.roll(r, QK_ROPE_DIM, axis=1)
        kf = jnp.where(is_nope, 0.0, kf)
        for hh in range(MLA_HEADS):
            sl = slice(hh * HEAD_LANES, (hh + 1) * HEAD_LANES)
            k_ref[0, r0:r1, sl] = (knope[:, sl] + kf).astype(BF16)
        vt_ref[0, :, r0:r1] = lax.dot_general(wvt_ref[...], ckv, (((1,), (1,)), ((), ())),
                                              preferred_element_type=F32).astype(BF16)

        gate_b = proj[:, c2:c2 + CONV_WIDTH]
        u = proj[:, c2 + CONV_WIDTH:c2 + 2 * CONV_WIDTH] * proj[:, c2 + 2 * CONV_WIDTH:c2 + 3 * CONV_WIDTH]
        cat = jnp.concatenate([prev, u], axis=0)
        u1 = pltpu.roll(cat, 1, axis=0)[8:]
        u2 = pltpu.roll(cat, 2, axis=0)[8:]
        prev = u[r1 - r0 - 8:]
        if r1 == t:
            carry_ref[...] = prev
        y = cb_ref[...] + cw_ref[0:1, :] * u2 + cw_ref[1:2, :] * u1 + cw_ref[2:3, :] * u
        y = gate_b * y
        for cc in range(CONV_WIDTH // LANES):
            sl = slice(cc * LANES, (cc + 1) * LANES)
            yv = y[:, sl]
            sq = yv * yv
            ms_lo = jnp.sum(jnp.where(lo, sq, 0.0), axis=-1, keepdims=True)
            ms_hi = jnp.sum(jnp.where(lo, 0.0, sq), axis=-1, keepdims=True)
            ms = jnp.where(lo, ms_lo, ms_hi) * (1.0 / (CONV_WIDTH // CONV_GROUPS))
            yc_ref[0, r0:r1, sl] = (yv * lax.rsqrt(ms + RMS_EPS) * og_ref[:, sl]).astype(BF16)


def _mixer_proj(x, mod, tab, win, qg, wq, kvg, wk, wvt, cw, cb, og):
    b, s, d = x.shape
    t = PROJ_TILE
    nt = s // t
    const = lambda shape: pl.BlockSpec(shape, lambda bi, si: (0,) * len(shape))
    return pl.pallas_call(
        _proj_kernel,
        grid=(b, nt),
        in_specs=[pl.BlockSpec((1, t, d), lambda bi, si: (bi, si, 0)),
                  pl.BlockSpec((1, 6, d), lambda bi, si: (bi, 0, 0)),
                  pl.BlockSpec((1, t, LANES), lambda bi, si: (bi, si, 0)),
                  const(win.shape), const(qg.shape), const(wq.shape), const(kvg.shape),
                  const(wk.shape), const(wvt.shape), const(cw.shape), const(cb.shape), const(og.shape)],
        out_specs=[pl.BlockSpec((1, t, MLA_HEADS * HEAD_LANES), lambda bi, si: (bi, si, 0)),
                   pl.BlockSpec((1, t, MLA_HEADS * HEAD_LANES), lambda bi, si: (bi, si, 0)),
                   pl.BlockSpec((1, ATTN_WIDTH, t), lambda bi, si: (bi, 0, si)),
                   pl.BlockSpec((1, t, CONV_WIDTH), lambda bi, si: (bi, si, 0))],
        out_shape=[jax.ShapeDtypeStruct((b, s, MLA_HEADS * HEAD_LANES), BF16),
                   jax.ShapeDtypeStruct((b, s, MLA_HEADS * HEAD_LANES), BF16),
                   jax.ShapeDtypeStruct((b, ATTN_WIDTH, s), BF16),
                   jax.ShapeDtypeStruct((b, s, CONV_WIDTH), BF16)],
        scratch_shapes=[pltpu.VMEM((8, CONV_WIDTH), F32)],
        compiler_params=pltpu.CompilerParams(dimension_semantics=("arbitrary", "arbitrary"),
                                             vmem_limit_bytes=56 * 1024 * 1024, flags=SCHED_FLAGS),
        name="mixer_proj",
    )(x, mod, tab, win, qg, wq, kvg, wk, wvt, cw, cb, og)


def _attn_kernel(q_ref, k_ref, vt_ref, g_ref, o_ref, s_ref, ot_ref):
    seq = q_ref.shape[1]
    qry_chunk = lax.broadcasted_iota(jnp.int32, (1, Q_TILE), 1) // CHUNK
    ones = jnp.ones((16, K_TILE), BF16)
    dn = (((1,), (1,)), ((), ()))

    def scores(qi, hh):
        q0 = qi * Q_TILE
        hl = slice(hh * HEAD_LANES, (hh + 1) * HEAD_LANES)
        q = q_ref[0, q0:q0 + Q_TILE, hl]
        if qi > 0:
            s_ref[hh, 0:q0, :] = lax.dot_general(k_ref[0, 0:q0, hl], q, dn, preferred_element_type=F32)
        for half in range(2):
            nk = (half + 1) * HALF
            sd = lax.dot_general(k_ref[0, q0:q0 + nk, hl], q[half * HALF:(half + 1) * HALF], dn,
                                 preferred_element_type=F32)
            cols = slice(half * HALF, (half + 1) * HALF)
            for kc in range(nk // CHUNK):
                rows = slice(kc * CHUNK, (kc + 1) * CHUNK)
                vis = qry_chunk[:, cols] >= kc
                s_ref[hh, q0 + kc * CHUNK:q0 + (kc + 1) * CHUNK, cols] = jnp.where(vis, sd[rows], NEG)

    def softmax_pv(qi, hh):
        kmax = (qi + 1) * Q_TILE
        n_full = kmax // K_TILE - 1
        last = slice(n_full * K_TILE, kmax)
        right = slice(HALF, Q_TILE)
        m = None
        for j in range(n_full):
            mj = jnp.max(s_ref[hh, j * K_TILE:(j + 1) * K_TILE, :], axis=0, keepdims=True)
            m = mj if m is None else jnp.maximum(m, mj)
        m_last = jnp.max(s_ref[hh, last, right], axis=0, keepdims=True)
        m = jnp.concatenate([m[:, :HALF], jnp.maximum(m[:, right], m_last)], axis=1)
        o = None
        for j in range(n_full):
            rows = slice(j * K_TILE, (j + 1) * K_TILE)
            p = jnp.exp2(s_ref[hh, rows, :] - m).astype(BF16)
            lhs = jnp.concatenate([vt_ref[0, hh * V_HEAD_DIM:(hh + 1) * V_HEAD_DIM, rows], ones], axis=0)
            oj = jnp.dot(lhs, p, preferred_element_type=F32)
            o = oj if o is None else o + oj
        p = jnp.exp2(s_ref[hh, last, right] - m[:, right]).astype(BF16)
        lhs = jnp.concatenate([vt_ref[0, hh * V_HEAD_DIM:(hh + 1) * V_HEAD_DIM, last], ones], axis=0)
        o = jnp.concatenate([o[:, :HALF], o[:, right] + jnp.dot(lhs, p, preferred_element_type=F32)], axis=1)
        o = o[0:V_HEAD_DIM] / o[V_HEAD_DIM:V_HEAD_DIM + 1]
        ms = jnp.mean(o * o, axis=0, keepdims=True)
        ot_ref[hh * V_HEAD_DIM:(hh + 1) * V_HEAD_DIM, :] = o * lax.rsqrt(ms + RMS_EPS)

    chains = [(qi, hh) for qi in range(seq // Q_TILE) for hh in range(2)]
    scores(*chains[0])
    for n, (qi, hh) in enumerate(chains):
        if n + 1 < len(chains):
            scores(*chains[n + 1])
        softmax_pv(qi, hh)
        if hh == 1:
            o_ref[0, qi * Q_TILE:(qi + 1) * Q_TILE, :] = (ot_ref[...].T * g_ref[...]).astype(BF16)


def _attention(q, k, vt, g):
    b, s, _ = q.shape
    pairs = MLA_HEADS // 2
    return pl.pallas_call(
        _attn_kernel,
        grid=(b, pairs),
        in_specs=[pl.BlockSpec((1, s, 2 * HEAD_LANES), lambda bi, pi: (bi, 0, pi)),
                  pl.BlockSpec((1, s, 2 * HEAD_LANES), lambda bi, pi: (bi, 0, pi)),
                  pl.BlockSpec((1, 2 * V_HEAD_DIM, s), lambda bi, pi: (bi, pi, 0)),
                  pl.BlockSpec((1, 2 * V_HEAD_DIM), lambda bi, pi: (0, pi))],
        out_specs=pl.BlockSpec((1, s, 2 * V_HEAD_DIM), lambda bi, pi: (bi, 0, pi)),
        out_shape=jax.ShapeDtypeStruct((b, s, ATTN_WIDTH), BF16),
        scratch_shapes=[pltpu.VMEM((2, s, Q_TILE), F32),
                        pltpu.VMEM((2 * V_HEAD_DIM, Q_TILE), F32)],
        compiler_params=pltpu.CompilerParams(dimension_semantics=("arbitrary",) * 2,
                                             vmem_limit_bytes=48 * 1024 * 1024, flags=SCHED_FLAGS),
        name="attention",
    )(q, k, vt, g)


def _ffn_kernel(x_ref, ya_ref, yc_ref, mod_ref, woa_ref, woc_ref, ln1g_ref, ln1b_ref, wup_ref,
                fcw_ref, fcb_ref, wdn_ref, ln2g_ref, ln2b_ref, o_ref, carry_ref, act_ref):
    t = TOK_TILE
    si = pl.program_id(1)
    gate_m = mod_ref[0, 2:3, :]
    shift_f = mod_ref[0, 3:4, :]
    scale_f = mod_ref[0, 4:5, :]
    gate_f = mod_ref[0, 5:6, :]

    @pl.when(si == 0)
    def _():
        carry_ref[...] = jnp.zeros(carry_ref.shape, F32)

    blocks = [(r, r + t // FFN_ROW_BLOCKS) for r in range(0, t, t // FFN_ROW_BLOCKS)]
    x1s, hs = [], []
    for r0, r1 in blocks:
        mix = (jnp.dot(ya_ref[0, r0:r1, :], woa_ref[...], preferred_element_type=F32)
               + jnp.dot(yc_ref[0, r0:r1, :], woc_ref[...], preferred_element_type=F32))
        x1 = _layer_norm(DEEPNORM_ALPHA * x_ref[0, r0:r1, :] + gate_m * mix) * ln1g_ref[...] + ln1b_ref[...]
        x1s.append(x1)
        hs.append((_layer_norm(x1) * (1.0 + scale_f) + shift_f).astype(BF16))

    def conv_cols(cols, h, prev, last):
        u = jnp.dot(h, wup_ref[:, cols], preferred_element_type=F32)
        cat = jnp.concatenate([prev, u], axis=0)
        u1 = pltpu.roll(cat, 1, axis=0)[8:]
        u2 = pltpu.roll(cat, 2, axis=0)[8:]
        tail = u[u.shape[0] - 8:]
        if last:
            carry_ref[:, cols] = tail
        y = fcb_ref[:, cols] + fcw_ref[0:1, cols] * u2 + fcw_ref[1:2, cols] * u1 + fcw_ref[2:3, cols] * u
        return y, tail

    for c in range(N_FF_CHUNKS):
        gcols = slice(c * FF_CHUNK, (c + 1) * FF_CHUNK)
        vcols = slice(D_FF + c * FF_CHUNK, D_FF + (c + 1) * FF_CHUNK)
        gprev, vprev = carry_ref[:, gcols], carry_ref[:, vcols]
        for bi, ((r0, r1), h) in enumerate(zip(blocks, hs)):
            last = bi == len(blocks) - 1
            g, gprev = conv_cols(gcols, h, gprev, last)
            v, vprev = conv_cols(vcols, h, vprev, last)
            act_ref[r0:r1, c * FF_CHUNK:(c + 1) * FF_CHUNK] = (g * _sigmoid(g) * v).astype(BF16)

    for (r0, r1), x1 in zip(blocks, x1s):
        ff = jnp.dot(act_ref[r0:r1, :], wdn_ref[...], preferred_element_type=F32)
        o_ref[0, r0:r1, :] = (_layer_norm(DEEPNORM_ALPHA * x1 + gate_f * ff) * ln2g_ref[...]
                              + ln2b_ref[...])


def _out_ffn(x, ya, yc, mod, woa, woc, ln1g, ln1b, wup, fcw, fcb, wdn, ln2g, ln2b):
    b, s, d = x.shape
    t = TOK_TILE
    const = lambda shape: pl.BlockSpec(shape, lambda bi, si: (0,) * len(shape),
                                       pipeline_mode=pl.Buffered(1))
    tok = lambda width: pl.BlockSpec((1, t, width), lambda bi, si: (bi, si, 0))
    return pl.pallas_call(
        _ffn_kernel,
        grid=(b, s // t),
        in_specs=[tok(d), tok(ATTN_WIDTH), tok(CONV_WIDTH),
                  pl.BlockSpec((1, 6, d), lambda bi, si: (bi, 0, 0)),
                  const(woa.shape), const(woc.shape), const(ln1g.shape), const(ln1b.shape),
                  const(wup.shape), const(fcw.shape), const(fcb.shape), const(wdn.shape),
                  const(ln2g.shape), const(ln2b.shape)],
        out_specs=tok(d),
        out_shape=jax.ShapeDtypeStruct((b, s, d), F32),
        scratch_shapes=[pltpu.VMEM((8, 2 * D_FF), F32),
                        pltpu.VMEM((t, D_FF), BF16)],
        compiler_params=pltpu.CompilerParams(dimension_semantics=("arbitrary", "arbitrary"),
                                             vmem_limit_bytes=56 * 1024 * 1024, flags=SCHED_FLAGS),
        name="out_ffn",
    )(x, ya, yc, mod, woa, woc, ln1g, ln1b, wup, fcw, fcb, wdn, ln2g, ln2b)


def _swap_halves(w):
    half = w.shape[-1] // 2
    return jnp.concatenate([w[..., half:], w[..., :half]], axis=-1)


def _prep_in_proj(w_in):
    c0 = Q_LORA_RANK + KV_LORA_RANK
    kr = w_in[:, c0:c0 + QK_ROPE_DIM]
    krs = _swap_halves(kr)
    return jnp.concatenate([w_in[:, :c0], kr, krs, kr, krs, w_in[:, c0 + QK_ROPE_DIM:]], axis=1).astype(BF16)


def _prep_q_up(w_q_up):
    w = w_q_up.reshape(Q_LORA_RANK, MLA_HEADS, QK_NOPE_DIM + QK_ROPE_DIM)
    rope = w[..., QK_NOPE_DIM:]
    w = jnp.concatenate([w[..., :QK_NOPE_DIM], rope, _swap_halves(rope)], axis=-1)
    return w.reshape(Q_LORA_RANK, MLA_HEADS * HEAD_LANES).astype(BF16)


def _prep_kv_up(w_kv_up):
    w = w_kv_up.reshape(KV_LORA_RANK, MLA_HEADS, QK_NOPE_DIM + V_HEAD_DIM)
    knope = w[..., :QK_NOPE_DIM]
    wk = jnp.concatenate([knope, jnp.zeros_like(knope)], axis=-1).reshape(KV_LORA_RANK, MLA_HEADS * HEAD_LANES)
    wvt = w[..., QK_NOPE_DIM:].reshape(KV_LORA_RANK, ATTN_WIDTH).T
    return wk.astype(BF16), wvt.astype(BF16)


def kernel(x, c, positions, w_ada, b_ada, w_in, q_norm_g, w_q_up, kv_norm_g, w_kv_up, conv_w, conv_b,
           out_norm_g, w_out, ln1_g, ln1_b, w_up, ffn_conv_w, ffn_conv_b, w_down, ln2_g, ln2_b):
    b, s, d = x.shape
    depth = w_ada.shape[0]
    tab = _rope_table(positions)
    for l in range(depth):
        mod = _adaln_mod(c, w_ada[l], b_ada[l]).reshape(b, 6, d)
        wk, wvt = _prep_kv_up(w_kv_up[l])
        q, k, vt, yc = _mixer_proj(
            x, mod, tab, _prep_in_proj(w_in[l]), q_norm_g[l].reshape(1, -1), _prep_q_up(w_q_up[l]),
            kv_norm_g[l].reshape(1, -1), wk, wvt, conv_w[l], conv_b[l].reshape(1, -1),
            out_norm_g[l, ATTN_WIDTH:].reshape(1, -1))
        ya = _attention(q, k, vt, out_norm_g[l, :ATTN_WIDTH].reshape(1, -1))
        x = _out_ffn(
            x, ya, yc, mod, w_out[l, :ATTN_WIDTH].astype(BF16), w_out[l, ATTN_WIDTH:].astype(BF16),
            ln1_g[l].reshape(1, -1), ln1_b[l].reshape(1, -1), w_up[l].astype(BF16),
            ffn_conv_w[l], ffn_conv_b[l].reshape(1, -1),
            w_down[l].astype(BF16), ln2_g[l].reshape(1, -1), ln2_b[l].reshape(1, -1))
    return x
```

```python
import jax
import jax.numpy as jnp
from jax import lax
from jax.experimental import pallas as pl
from jax.experimental.pallas import tpu as pltpu

D_MODEL = 1024
SEQ = 2048
CHUNK = 64
MLA_HEADS = 8
QK_NOPE_DIM = 64
QK_ROPE_DIM = 32
V_HEAD_DIM = 64
Q_LORA_RANK = 256
KV_LORA_RANK = 128
ATTN_WIDTH = MLA_HEADS * V_HEAD_DIM
CONV_WIDTH = D_MODEL - ATTN_WIDTH
CONV_GROUPS = 8
D_FF = 2816
ROPE_THETA = 10000.0
RMS_EPS = 1e-6
LN_EPS = 1e-5
DEEPNORM_ALPHA = 2.0 ** 0.25

LANES = 128
HEAD_LANES = 128
IN_EXT = Q_LORA_RANK + KV_LORA_RANK + LANES + 3 * CONV_WIDTH
TOK_TILE = 512
PROJ_TILE = 1024
Q_TILE = 512
K_TILE = 256
HALF = Q_TILE // 2
assert K_TILE == HALF
FF_CHUNK = 256
N_FF_CHUNKS = D_FF // FF_CHUNK
FFN_ROW_BLOCKS = 2
PROJ_ROW_BLOCKS = 4
ADALN_K_BLOCK = 256
NEG = -1e30
Q_SCALE = (QK_NOPE_DIM + QK_ROPE_DIM) ** -0.5 * 1.4426950408889634

F32 = jnp.float32
BF16 = jnp.bfloat16
SCHED_FLAGS = None


def _layer_norm(x):
    mu = jnp.mean(x, axis=-1, keepdims=True)
    xc = x - mu
    var = jnp.mean(xc * xc, axis=-1, keepdims=True)
    return xc * lax.rsqrt(var + LN_EPS)


def _rms(x):
    return x * lax.rsqrt(jnp.mean(x * x, axis=-1, keepdims=True) + RMS_EPS)


def _sigmoid(x):
    return 1.0 / (1.0 + jnp.exp(-x))


def _split3(x):
    hi = x.astype(BF16)
    r1 = x - hi.astype(F32)
    mid = r1.astype(BF16)
    lo = (r1 - mid.astype(F32)).astype(BF16)
    return hi, mid, lo


def _rope_kernel(pos_ref, freq_ref, e_ref, tab_ref):
    ang = freq_ref[...] * pos_ref[0].astype(F32)
    pieces = jnp.concatenate(_split3(jnp.cos(ang)) + _split3(jnp.sin(ang)), axis=0)
    tab_ref[0] = lax.dot_general(pieces, e_ref[...], (((0,), (0,)), ((), ())),
                                 preferred_element_type=F32)


def _rope_expansion():
    half = QK_ROPE_DIM // 2
    src = jnp.arange(6 * half)[:, None]
    dst = jnp.arange(LANES)[None, :]
    same_freq = src % half == dst % half
    is_sin_row = src >= 3 * half
    group = dst // half % 4
    e_cos = jnp.where(same_freq & ~is_sin_row & (group < 2), 1.0, 0.0)
    e_sin = jnp.where(same_freq & is_sin_row & (group == 2), -1.0,
                      jnp.where(same_freq & is_sin_row & (group == 3), 1.0, 0.0))
    return (e_cos + e_sin).astype(BF16)


def _rope_table(positions):
    b, s = positions.shape
    half = QK_ROPE_DIM // 2
    inv_freq = ROPE_THETA ** (-jnp.arange(0, QK_ROPE_DIM, 2, dtype=F32) / QK_ROPE_DIM)
    e = _rope_expansion()
    return pl.pallas_call(
        _rope_kernel,
        grid=(b,),
        in_specs=[pl.BlockSpec((1, 1, s), lambda i: (i, 0, 0)),
                  pl.BlockSpec((half, 1), lambda i: (0, 0)),
                  pl.BlockSpec(e.shape, lambda i: (0, 0))],
        out_specs=pl.BlockSpec((1, s, LANES), lambda i: (i, 0, 0)),
        out_shape=jax.ShapeDtypeStruct((b, s, LANES), F32),
        compiler_params=pltpu.CompilerParams(dimension_semantics=("arbitrary",),
                                             vmem_limit_bytes=32 * 1024 * 1024),
        name="rope_table",
    )(positions.reshape(b, 1, s), inv_freq.reshape(half, 1), e)


def _mod_kernel(c_ref, w_ref, b_ref, o_ref):
    @pl.when(pl.program_id(0) == 0)
    def _():
        o_ref[...] = jnp.broadcast_to(b_ref[...], o_ref.shape)

    c = c_ref[...]
    act = c * _sigmoid(c)
    o_ref[...] += jnp.dot(act, w_ref[...], preferred_element_type=F32, precision=lax.Precision.HIGHEST)


def _adaln_mod(c, w_ada, b_ada):
    b, d = c.shape
    n = w_ada.shape[1]
    kb = ADALN_K_BLOCK
    return pl.pallas_call(
        _mod_kernel,
        grid=(d // kb,),
        in_specs=[pl.BlockSpec((b, kb), lambda j: (0, j)),
                  pl.BlockSpec((kb, n), lambda j: (j, 0)),
                  pl.BlockSpec((1, n), lambda j: (0, 0))],
        out_specs=pl.BlockSpec((b, n), lambda j: (0, 0)),
        out_shape=jax.ShapeDtypeStruct((b, n), F32),
        compiler_params=pltpu.CompilerParams(dimension_semantics=("arbitrary",),
                                             vmem_limit_bytes=32 * 1024 * 1024),
        name="adaln_mod",
    )(c, w_ada, b_ada.reshape(1, n))


def _proj_kernel(x_ref, mod_ref, tab_ref, win_ref, qg_ref, wq_ref, kvg_ref, wk_ref, wvt_ref,
                 cw_ref, cb_ref, og_ref, q_ref, k_ref, vt_ref, yc_ref, carry_ref):
    t = PROJ_TILE
    si = pl.program_id(1)
    shift = mod_ref[0, 0:1, :]
    scale = mod_ref[0, 1:2, :]
    lane = lax.broadcasted_iota(jnp.int32, (1, LANES), 1)
    is_nope = lane < QK_NOPE_DIM
    lo = lane < (CONV_WIDTH // CONV_GROUPS)
    c0 = Q_LORA_RANK
    c1 = c0 + KV_LORA_RANK
    c2 = c1 + LANES

    @pl.when(si == 0)
    def _():
        carry_ref[...] = jnp.zeros((8, CONV_WIDTH), F32)

    prev = carry_ref[...]
    for r0 in range(0, t, t // PROJ_ROW_BLOCKS):
        r1 = r0 + t // PROJ_ROW_BLOCKS
        h = (_layer_norm(x_ref[0, r0:r1, :]) * (1.0 + scale) + shift).astype(BF16)
        proj = jnp.dot(h, win_ref[...], preferred_element_type=F32)
        tab = tab_ref[0, r0:r1, :]

        cq = (_rms(proj[:, 0:Q_LORA_RANK]) * qg_ref[...]).astype(BF16)
        q = jnp.dot(cq, wq_ref[...], preferred_element_type=F32)
        q_tab = jnp.where(is_nope, Q_SCALE, Q_SCALE * tab)
        for hh in range(MLA_HEADS):
            sl = slice(hh * HEAD_LANES, (hh + 1) * HEAD_LANES)
            q_ref[0, r0:r1, sl] = (q[:, sl] * q_tab).astype(BF16)

        ckv = (_rms(proj[:, c0:c0 + KV_LORA_RANK]) * kvg_ref[...]).astype(BF16)
        knope = jnp.dot(ckv, wk_ref[...], preferred_element_type=F32)
        r = proj[:, c1:c1 + LANES] * tab
        kf = r + pltpu.roll(r, QK_ROPE_DIM, axis=1)
        for hp in range(MLA_HEADS // 2):
            pair = knope[:, hp * LANES:(hp + 1) * LANES]
            for hh, nope in ((2 * hp, pair), (2 * hp + 1, pltpu.roll(pair, QK_NOPE_DIM, axis=1))):
                sl = slice(hh * HEAD_LANES, (hh + 1) * HEAD_LANES)
                k_ref[0, r0:r1, sl] = jnp.where(is_nope, nope, kf).astype(BF16)
        vt_ref[0, :, r0:r1] = lax.dot_general(wvt_ref[...], ckv, (((1,), (1,)), ((), ())),
                                              preferred_element_type=F32).astype(BF16)

        gate_b = proj[:, c2:c2 + CONV_WIDTH]
        u = proj[:, c2 + CONV_WIDTH:c2 + 2 * CONV_WIDTH] * proj[:, c2 + 2 * CONV_WIDTH:c2 + 3 * CONV_WIDTH]
        cat = jnp.concatenate([prev, u], axis=0)
        u1 = pltpu.roll(cat, 1, axis=0)[8:]
        u2 = pltpu.roll(cat, 2, axis=0)[8:]
        prev = u[r1 - r0 - 8:]
        if r1 == t:
            carry_ref[...] = prev
        y = cb_ref[...] + cw_ref[0:1, :] * u2 + cw_ref[1:2, :] * u1 + cw_ref[2:3, :] * u
        y = gate_b * y
        for cc in range(CONV_WIDTH // LANES):
            sl = slice(cc * LANES, (cc + 1) * LANES)
            yv = y[:, sl]
            sq = yv * yv
            ms_lo = jnp.sum(jnp.where(lo, sq, 0.0), axis=-1, keepdims=True)
            ms_hi = jnp.sum(jnp.where(lo, 0.0, sq), axis=-1, keepdims=True)
            ms = jnp.where(lo, ms_lo, ms_hi) * (1.0 / (CONV_WIDTH // CONV_GROUPS))
            yc_ref[0, r0:r1, sl] = (yv * lax.rsqrt(ms + RMS_EPS) * og_ref[:, sl]).astype(BF16)


def _mixer_proj(x, mod, tab, win, qg, wq, kvg, wk, wvt, cw, cb, og):
    b, s, d = x.shape
    t = PROJ_TILE
    nt = s // t
    const = lambda shape: pl.BlockSpec(shape, lambda bi, si: (0,) * len(shape))
    return pl.pallas_call(
        _proj_kernel,
        grid=(b, nt),
        in_specs=[pl.BlockSpec((1, t, d), lambda bi, si: (bi, si, 0)),
                  pl.BlockSpec((1, 6, d), lambda bi, si: (bi, 0, 0)),
                  pl.BlockSpec((1, t, LANES), lambda bi, si: (bi, si, 0)),
                  const(win.shape), const(qg.shape), const(wq.shape), const(kvg.shape),
                  const(wk.shape), const(wvt.shape), const(cw.shape), const(cb.shape), const(og.shape)],
        out_specs=[pl.BlockSpec((1, t, MLA_HEADS * HEAD_LANES), lambda bi, si: (bi, si, 0)),
                   pl.BlockSpec((1, t, MLA_HEADS * HEAD_LANES), lambda bi, si: (bi, si, 0)),
                   pl.BlockSpec((1, ATTN_WIDTH, t), lambda bi, si: (bi, 0, si)),
                   pl.BlockSpec((1, t, CONV_WIDTH), lambda bi, si: (bi, si, 0))],
        out_shape=[jax.ShapeDtypeStruct((b, s, MLA_HEADS * HEAD_LANES), BF16),
                   jax.ShapeDtypeStruct((b, s, MLA_HEADS * HEAD_LANES), BF16),
                   jax.ShapeDtypeStruct((b, ATTN_WIDTH, s), BF16),
                   jax.ShapeDtypeStruct((b, s, CONV_WIDTH), BF16)],
        scratch_shapes=[pltpu.VMEM((8, CONV_WIDTH), F32)],
        compiler_params=pltpu.CompilerParams(dimension_semantics=("arbitrary", "arbitrary"),
                                             vmem_limit_bytes=56 * 1024 * 1024, flags=SCHED_FLAGS),
        name="mixer_proj",
    )(x, mod, tab, win, qg, wq, kvg, wk, wvt, cw, cb, og)


def _attn_kernel(q_ref, k_ref, vt_ref, g_ref, o_ref, s_ref, ot_ref):
    seq = q_ref.shape[1]
    qry_chunk = lax.broadcasted_iota(jnp.int32, (1, Q_TILE), 1) // CHUNK
    ones = jnp.ones((16, K_TILE), BF16)
    dn = (((1,), (1,)), ((), ()))

    def scores(qi, hh):
        q0 = qi * Q_TILE
        hl = slice(hh * HEAD_LANES, (hh + 1) * HEAD_LANES)
        q = q_ref[0, q0:q0 + Q_TILE, hl]
        if qi > 0:
            s_ref[hh, 0:q0, :] = lax.dot_general(k_ref[0, 0:q0, hl], q, dn, preferred_element_type=F32)
        for half in range(2):
            nk = (half + 1) * HALF
            sd = lax.dot_general(k_ref[0, q0:q0 + nk, hl], q[half * HALF:(half + 1) * HALF], dn,
                                 preferred_element_type=F32)
            cols = slice(half * HALF, (half + 1) * HALF)
            for kc in range(nk // CHUNK):
                rows = slice(kc * CHUNK, (kc + 1) * CHUNK)
                vis = qry_chunk[:, cols] >= kc
                s_ref[hh, q0 + kc * CHUNK:q0 + (kc + 1) * CHUNK, cols] = jnp.where(vis, sd[rows], NEG)

    def softmax_pv(qi, hh):
        kmax = (qi + 1) * Q_TILE
        n_full = kmax // K_TILE - 1
        last = slice(n_full * K_TILE, kmax)
        right = slice(HALF, Q_TILE)
        m = None
        for j in range(n_full):
            mj = jnp.max(s_ref[hh, j * K_TILE:(j + 1) * K_TILE, :], axis=0, keepdims=True)
            m = mj if m is None else jnp.maximum(m, mj)
        m_last = jnp.max(s_ref[hh, last, right], axis=0, keepdims=True)
        m = jnp.concatenate([m[:, :HALF], jnp.maximum(m[:, right], m_last)], axis=1)
        o = None
        for j in range(n_full):
            rows = slice(j * K_TILE, (j + 1) * K_TILE)
            p = jnp.exp2(s_ref[hh, rows, :] - m).astype(BF16)
            lhs = jnp.concatenate([vt_ref[0, hh * V_HEAD_DIM:(hh + 1) * V_HEAD_DIM, rows], ones], axis=0)
            oj = jnp.dot(lhs, p, preferred_element_type=F32)
            o = oj if o is None else o + oj
        p = jnp.exp2(s_ref[hh, last, right] - m[:, right]).astype(BF16)
        lhs = jnp.concatenate([vt_ref[0, hh * V_HEAD_DIM:(hh + 1) * V_HEAD_DIM, last], ones], axis=0)
        o = jnp.concatenate([o[:, :HALF], o[:, right] + jnp.dot(lhs, p, preferred_element_type=F32)], axis=1)
        o = o[0:V_HEAD_DIM] / o[V_HEAD_DIM:V_HEAD_DIM + 1]
        ms = jnp.mean(o * o, axis=0, keepdims=True)
        ot_ref[hh * V_HEAD_DIM:(hh + 1) * V_HEAD_DIM, :] = o * lax.rsqrt(ms + RMS_EPS)

    chains = [(qi, hh) for qi in range(seq // Q_TILE) for hh in range(2)]
    scores(*chains[0])
    for n, (qi, hh) in enumerate(chains):
        if n + 1 < len(chains):
            scores(*chains[n + 1])
        softmax_pv(qi, hh)
        if hh == 1:
            o_ref[0, qi * Q_TILE:(qi + 1) * Q_TILE, :] = (ot_ref[...].T * g_ref[...]).astype(BF16)


def _attention(q, k, vt, g):
    b, s, _ = q.shape
    pairs = MLA_HEADS // 2
    return pl.pallas_call(
        _attn_kernel,
        grid=(b, pairs),
        in_specs=[pl.BlockSpec((1, s, 2 * HEAD_LANES), lambda bi, pi: (bi, 0, pi)),
                  pl.BlockSpec((1, s, 2 * HEAD_LANES), lambda bi, pi: (bi, 0, pi)),
                  pl.BlockSpec((1, 2 * V_HEAD_DIM, s), lambda bi, pi: (bi, pi, 0)),
                  pl.BlockSpec((1, 2 * V_HEAD_DIM), lambda bi, pi: (0, pi))],
        out_specs=pl.BlockSpec((1, s, 2 * V_HEAD_DIM), lambda bi, pi: (bi, 0, pi)),
        out_shape=jax.ShapeDtypeStruct((b, s, ATTN_WIDTH), BF16),
        scratch_shapes=[pltpu.VMEM((2, s, Q_TILE), F32),
                        pltpu.VMEM((2 * V_HEAD_DIM, Q_TILE), F32)],
        compiler_params=pltpu.CompilerParams(dimension_semantics=("arbitrary",) * 2,
                                             vmem_limit_bytes=48 * 1024 * 1024, flags=SCHED_FLAGS),
        name="attention",
    )(q, k, vt, g)


def _ffn_kernel(x_ref, ya_ref, yc_ref, mod_ref, woa_ref, woc_ref, ln1g_ref, ln1b_ref, wup_ref,
                fcw_ref, fcb_ref, wdn_ref, ln2g_ref, ln2b_ref, o_ref, carry_ref, act_ref):
    t = TOK_TILE
    si = pl.program_id(1)
    gate_m = mod_ref[0, 2:3, :]
    shift_f = mod_ref[0, 3:4, :]
    scale_f = mod_ref[0, 4:5, :]
    gate_f = mod_ref[0, 5:6, :]

    @pl.when(si == 0)
    def _():
        carry_ref[...] = jnp.zeros(carry_ref.shape, F32)

    blocks = [(r, r + t // FFN_ROW_BLOCKS) for r in range(0, t, t // FFN_ROW_BLOCKS)]
    x1s, hs = [], []
    for r0, r1 in blocks:
        mix = (jnp.dot(ya_ref[0, r0:r1, :], woa_ref[...], preferred_element_type=F32)
               + jnp.dot(yc_ref[0, r0:r1, :], woc_ref[...], preferred_element_type=F32))
        x1 = _layer_norm(DEEPNORM_ALPHA * x_ref[0, r0:r1, :] + gate_m * mix) * ln1g_ref[...] + ln1b_ref[...]
        x1s.append(x1)
        hs.append((_layer_norm(x1) * (1.0 + scale_f) + shift_f).astype(BF16))

    def conv_cols(cols, h, prev, last):
        u = jnp.dot(h, wup_ref[:, cols], preferred_element_type=F32)
        cat = jnp.concatenate([prev, u], axis=0)
        u1 = pltpu.roll(cat, 1, axis=0)[8:]
        u2 = pltpu.roll(cat, 2, axis=0)[8:]
        tail = u[u.shape[0] - 8:]
        if last:
            carry_ref[:, cols] = tail
        y = fcb_ref[:, cols] + fcw_ref[0:1, cols] * u2 + fcw_ref[1:2, cols] * u1 + fcw_ref[2:3, cols] * u
        return y, tail

    for c in range(N_FF_CHUNKS):
        gcols = slice(c * FF_CHUNK, (c + 1) * FF_CHUNK)
        vcols = slice(D_FF + c * FF_CHUNK, D_FF + (c + 1) * FF_CHUNK)
        gprev, vprev = carry_ref[:, gcols], carry_ref[:, vcols]
        for bi, ((r0, r1), h) in enumerate(zip(blocks, hs)):
            last = bi == len(blocks) - 1
            g, gprev = conv_cols(gcols, h, gprev, last)
            v, vprev = conv_cols(vcols, h, vprev, last)
            act_ref[r0:r1, c * FF_CHUNK:(c + 1) * FF_CHUNK] = (g * _sigmoid(g) * v).astype(BF16)

    for (r0, r1), x1 in zip(blocks, x1s):
        ff = jnp.dot(act_ref[r0:r1, :], wdn_ref[...], preferred_element_type=F32)
        o_ref[0, r0:r1, :] = (_layer_norm(DEEPNORM_ALPHA * x1 + gate_f * ff) * ln2g_ref[...]
                              + ln2b_ref[...])


def _out_ffn(x, ya, yc, mod, woa, woc, ln1g, ln1b, wup, fcw, fcb, wdn, ln2g, ln2b):
    b, s, d = x.shape
    t = TOK_TILE
    const = lambda shape: pl.BlockSpec(shape, lambda bi, si: (0,) * len(shape),
                                       pipeline_mode=pl.Buffered(1))
    tok = lambda width: pl.BlockSpec((1, t, width), lambda bi, si: (bi, si, 0))
    return pl.pallas_call(
        _ffn_kernel,
        grid=(b, s // t),
        in_specs=[tok(d), tok(ATTN_WIDTH), tok(CONV_WIDTH),
                  pl.BlockSpec((1, 6, d), lambda bi, si: (bi, 0, 0)),
                  const(woa.shape), const(woc.shape), const(ln1g.shape), const(ln1b.shape),
                  const(wup.shape), const(fcw.shape), const(fcb.shape), const(wdn.shape),
                  const(ln2g.shape), const(ln2b.shape)],
        out_specs=tok(d),
        out_shape=jax.ShapeDtypeStruct((b, s, d), F32),
        scratch_shapes=[pltpu.VMEM((8, 2 * D_FF), F32),
                        pltpu.VMEM((t, D_FF), BF16)],
        compiler_params=pltpu.CompilerParams(dimension_semantics=("arbitrary", "arbitrary"),
                                             vmem_limit_bytes=56 * 1024 * 1024, flags=SCHED_FLAGS),
        name="out_ffn",
    )(x, ya, yc, mod, woa, woc, ln1g, ln1b, wup, fcw, fcb, wdn, ln2g, ln2b)


def _swap_halves(w):
    half = w.shape[-1] // 2
    return jnp.concatenate([w[..., half:], w[..., :half]], axis=-1)


def _prep_in_proj(w_in):
    c0 = Q_LORA_RANK + KV_LORA_RANK
    kr = w_in[:, c0:c0 + QK_ROPE_DIM]
    krs = _swap_halves(kr)
    return jnp.concatenate([w_in[:, :c0], kr, krs, kr, krs, w_in[:, c0 + QK_ROPE_DIM:]], axis=1).astype(BF16)


def _prep_q_up(w_q_up):
    w = w_q_up.reshape(Q_LORA_RANK, MLA_HEADS, QK_NOPE_DIM + QK_ROPE_DIM)
    rope = w[..., QK_NOPE_DIM:]
    w = jnp.concatenate([w[..., :QK_NOPE_DIM], rope, _swap_halves(rope)], axis=-1)
    return w.reshape(Q_LORA_RANK, MLA_HEADS * HEAD_LANES).astype(BF16)


def _prep_kv_up(w_kv_up):
    w = w_kv_up.reshape(KV_LORA_RANK, MLA_HEADS, QK_NOPE_DIM + V_HEAD_DIM)
    wk = w[..., :QK_NOPE_DIM].reshape(KV_LORA_RANK, MLA_HEADS * QK_NOPE_DIM)
    wvt = w[..., QK_NOPE_DIM:].reshape(KV_LORA_RANK, ATTN_WIDTH).T
    return wk.astype(BF16), wvt.astype(BF16)


def kernel(x, c, positions, w_ada, b_ada, w_in, q_norm_g, w_q_up, kv_norm_g, w_kv_up, conv_w, conv_b,
           out_norm_g, w_out, ln1_g, ln1_b, w_up, ffn_conv_w, ffn_conv_b, w_down, ln2_g, ln2_b):
    b, s, d = x.shape
    depth = w_ada.shape[0]
    tab = _rope_table(positions)
    for l in range(depth):
        mod = _adaln_mod(c, w_ada[l], b_ada[l]).reshape(b, 6, d)
        wk, wvt = _prep_kv_up(w_kv_up[l])
        q, k, vt, yc = _mixer_proj(
            x, mod, tab, _prep_in_proj(w_in[l]), q_norm_g[l].reshape(1, -1), _prep_q_up(w_q_up[l]),
            kv_norm_g[l].reshape(1, -1), wk, wvt, conv_w[l], conv_b[l].reshape(1, -1),
            out_norm_g[l, ATTN_WIDTH:].reshape(1, -1))
        ya = _attention(q, k, vt, out_norm_g[l, :ATTN_WIDTH].reshape(1, -1))
        x = _out_ffn(
            x, ya, yc, mod, w_out[l, :ATTN_WIDTH].astype(BF16), w_out[l, ATTN_WIDTH:].astype(BF16),
            ln1_g[l].reshape(1, -1), ln1_b[l].reshape(1, -1), w_up[l].astype(BF16),
            ffn_conv_w[l], ffn_conv_b[l].reshape(1, -1),
            w_down[l].astype(BF16), ln2_g[l].reshape(1, -1), ln2_b[l].reshape(1, -1))
    return x
```

```python
import jax
import jax.numpy as jnp
from jax import lax
from jax.experimental import pallas as pl
from jax.experimental.pallas import tpu as pltpu

D_MODEL = 1024
SEQ = 2048
CHUNK = 64
MLA_HEADS = 8
QK_NOPE_DIM = 64
QK_ROPE_DIM = 32
V_HEAD_DIM = 64
Q_LORA_RANK = 256
KV_LORA_RANK = 128
ATTN_WIDTH = MLA_HEADS * V_HEAD_DIM
CONV_WIDTH = D_MODEL - ATTN_WIDTH
CONV_GROUPS = 8
D_FF = 2816
ROPE_THETA = 10000.0
RMS_EPS = 1e-6
LN_EPS = 1e-5
DEEPNORM_ALPHA = 2.0 ** 0.25

LANES = 128
HEAD_LANES = 128
IN_EXT = Q_LORA_RANK + KV_LORA_RANK + LANES + 3 * CONV_WIDTH
TOK_TILE = 512
PROJ_TILE = 1024
Q_TILE = 512
K_TILE = 256
HALF = Q_TILE // 2
assert K_TILE == HALF
FF_CHUNK = 512
FFN_ROW_BLOCKS = 2
PROJ_ROW_BLOCKS = 4
ADALN_K_BLOCK = 256
NEG = -1e30
Q_SCALE = (QK_NOPE_DIM + QK_ROPE_DIM) ** -0.5 * 1.4426950408889634

F32 = jnp.float32
BF16 = jnp.bfloat16
SCHED_FLAGS = None


def _layer_norm(x):
    mu = jnp.mean(x, axis=-1, keepdims=True)
    xc = x - mu
    var = jnp.mean(xc * xc, axis=-1, keepdims=True)
    return xc * lax.rsqrt(var + LN_EPS)


def _rms(x):
    return x * lax.rsqrt(jnp.mean(x * x, axis=-1, keepdims=True) + RMS_EPS)


def _sigmoid(x):
    return 1.0 / (1.0 + jnp.exp(-x))


def _split3(x):
    hi = x.astype(BF16)
    r1 = x - hi.astype(F32)
    mid = r1.astype(BF16)
    lo = (r1 - mid.astype(F32)).astype(BF16)
    return hi, mid, lo


def _rope_kernel(pos_ref, freq_ref, e_ref, tab_ref):
    ang = freq_ref[...] * pos_ref[0].astype(F32)
    pieces = jnp.concatenate(_split3(jnp.cos(ang)) + _split3(jnp.sin(ang)), axis=0)
    tab_ref[0] = lax.dot_general(pieces, e_ref[...], (((0,), (0,)), ((), ())),
                                 preferred_element_type=F32)


def _rope_expansion():
    half = QK_ROPE_DIM // 2
    src = jnp.arange(6 * half)[:, None]
    dst = jnp.arange(LANES)[None, :]
    same_freq = src % half == dst % half
    is_sin_row = src >= 3 * half
    group = dst // half % 4
    e_cos = jnp.where(same_freq & ~is_sin_row & (group < 2), 1.0, 0.0)
    e_sin = jnp.where(same_freq & is_sin_row & (group == 2), -1.0,
                      jnp.where(same_freq & is_sin_row & (group == 3), 1.0, 0.0))
    return (e_cos + e_sin).astype(BF16)


def _rope_table(positions):
    b, s = positions.shape
    half = QK_ROPE_DIM // 2
    inv_freq = ROPE_THETA ** (-jnp.arange(0, QK_ROPE_DIM, 2, dtype=F32) / QK_ROPE_DIM)
    e = _rope_expansion()
    return pl.pallas_call(
        _rope_kernel,
        grid=(b,),
        in_specs=[pl.BlockSpec((1, 1, s), lambda i: (i, 0, 0)),
                  pl.BlockSpec((half, 1), lambda i: (0, 0)),
                  pl.BlockSpec(e.shape, lambda i: (0, 0))],
        out_specs=pl.BlockSpec((1, s, LANES), lambda i: (i, 0, 0)),
        out_shape=jax.ShapeDtypeStruct((b, s, LANES), F32),
        compiler_params=pltpu.CompilerParams(dimension_semantics=("arbitrary",),
                                             vmem_limit_bytes=32 * 1024 * 1024),
        name="rope_table",
    )(positions.reshape(b, 1, s), inv_freq.reshape(half, 1), e)


def _mod_kernel(c_ref, w_ref, b_ref, o_ref):
    @pl.when(pl.program_id(0) == 0)
    def _():
        o_ref[...] = jnp.broadcast_to(b_ref[...], o_ref.shape)

    c = c_ref[...]
    act = c * _sigmoid(c)
    o_ref[...] += jnp.dot(act, w_ref[...], preferred_element_type=F32, precision=lax.Precision.HIGHEST)


def _adaln_mod(c, w_ada, b_ada):
    b, d = c.shape
    n = w_ada.shape[1]
    kb = ADALN_K_BLOCK
    return pl.pallas_call(
        _mod_kernel,
        grid=(d // kb,),
        in_specs=[pl.BlockSpec((b, kb), lambda j: (0, j)),
                  pl.BlockSpec((kb, n), lambda j: (j, 0)),
                  pl.BlockSpec((1, n), lambda j: (0, 0))],
        out_specs=pl.BlockSpec((b, n), lambda j: (0, 0)),
        out_shape=jax.ShapeDtypeStruct((b, n), F32),
        compiler_params=pltpu.CompilerParams(dimension_semantics=("arbitrary",),
                                             vmem_limit_bytes=32 * 1024 * 1024),
        name="adaln_mod",
    )(c, w_ada, b_ada.reshape(1, n))


def _proj_kernel(x_ref, mod_ref, tab_ref, win_ref, qg_ref, wq_ref, kvg_ref, wk_ref, wvt_ref,
                 cw_ref, cb_ref, og_ref, q_ref, k_ref, vt_ref, yc_ref, carry_ref):
    t = PROJ_TILE
    si = pl.program_id(1)
    shift = mod_ref[0, 0:1, :]
    scale = mod_ref[0, 1:2, :]
    lane = lax.broadcasted_iota(jnp.int32, (1, LANES), 1)
    is_nope = lane < QK_NOPE_DIM
    lo = lane < (CONV_WIDTH // CONV_GROUPS)
    c0 = Q_LORA_RANK
    c1 = c0 + KV_LORA_RANK
    c2 = c1 + LANES

    @pl.when(si == 0)
    def _():
        carry_ref[...] = jnp.zeros((8, CONV_WIDTH), F32)

    prev = carry_ref[...]
    for r0 in range(0, t, t // PROJ_ROW_BLOCKS):
        r1 = r0 + t // PROJ_ROW_BLOCKS
        h = (_layer_norm(x_ref[0, r0:r1, :]) * (1.0 + scale) + shift).astype(BF16)
        proj = jnp.dot(h, win_ref[...], preferred_element_type=F32)
        tab = tab_ref[0, r0:r1, :]

        cq = (_rms(proj[:, 0:Q_LORA_RANK]) * qg_ref[...]).astype(BF16)
        q = jnp.dot(cq, wq_ref[...], preferred_element_type=F32)
        q_tab = jnp.where(is_nope, Q_SCALE, Q_SCALE * tab)
        for hh in range(MLA_HEADS):
            sl = slice(hh * HEAD_LANES, (hh + 1) * HEAD_LANES)
            q_ref[0, r0:r1, sl] = (q[:, sl] * q_tab).astype(BF16)

        ckv = (_rms(proj[:, c0:c0 + KV_LORA_RANK]) * kvg_ref[...]).astype(BF16)
        knope = jnp.dot(ckv, wk_ref[...], preferred_element_type=F32)
        r = proj[:, c1:c1 + LANES] * tab
        kf = r + pltpu.roll(r, QK_ROPE_DIM, axis=1)
        for hp in range(MLA_HEADS // 2):
            pair = knope[:, hp * LANES:(hp + 1) * LANES]
            for hh, nope in ((2 * hp, pair), (2 * hp + 1, pltpu.roll(pair, QK_NOPE_DIM, axis=1))):
                sl = slice(hh * HEAD_LANES, (hh + 1) * HEAD_LANES)
                k_ref[0, r0:r1, sl] = jnp.where(is_nope, nope, kf).astype(BF16)
        vt_ref[0, :, r0:r1] = lax.dot_general(wvt_ref[...], ckv, (((1,), (1,)), ((), ())),
                                              preferred_element_type=F32).astype(BF16)

        gate_b = proj[:, c2:c2 + CONV_WIDTH]
        u = proj[:, c2 + CONV_WIDTH:c2 + 2 * CONV_WIDTH] * proj[:, c2 + 2 * CONV_WIDTH:c2 + 3 * CONV_WIDTH]
        cat = jnp.concatenate([prev, u], axis=0)
        u1 = pltpu.roll(cat, 1, axis=0)[8:]
        u2 = pltpu.roll(cat, 2, axis=0)[8:]
        prev = u[r1 - r0 - 8:]
        if r1 == t:
            carry_ref[...] = prev
        y = cb_ref[...] + cw_ref[0:1, :] * u2 + cw_ref[1:2, :] * u1 + cw_ref[2:3, :] * u
        y = gate_b * y
        for cc in range(CONV_WIDTH // LANES):
            sl = slice(cc * LANES, (cc + 1) * LANES)
            yv = y[:, sl]
            sq = yv * yv
            ms_lo = jnp.sum(jnp.where(lo, sq, 0.0), axis=-1, keepdims=True)
            ms_hi = jnp.sum(jnp.where(lo, 0.0, sq), axis=-1, keepdims=True)
            ms = jnp.where(lo, ms_lo, ms_hi) * (1.0 / (CONV_WIDTH // CONV_GROUPS))
            yc_ref[0, r0:r1, sl] = (yv * lax.rsqrt(ms + RMS_EPS) * og_ref[:, sl]).astype(BF16)


def _mixer_proj(x, mod, tab, win, qg, wq, kvg, wk, wvt, cw, cb, og):
    b, s, d = x.shape
    t = PROJ_TILE
    nt = s // t
    const = lambda shape: pl.BlockSpec(shape, lambda bi, si: (0,) * len(shape))
    return pl.pallas_call(
        _proj_kernel,
        grid=(b, nt),
        in_specs=[pl.BlockSpec((1, t, d), lambda bi, si: (bi, si, 0)),
                  pl.BlockSpec((1, 6, d), lambda bi, si: (bi, 0, 0)),
                  pl.BlockSpec((1, t, LANES), lambda bi, si: (bi, si, 0)),
                  const(win.shape), const(qg.shape), const(wq.shape), const(kvg.shape),
                  const(wk.shape), const(wvt.shape), const(cw.shape), const(cb.shape), const(og.shape)],
        out_specs=[pl.BlockSpec((1, t, MLA_HEADS * HEAD_LANES), lambda bi, si: (bi, si, 0)),
                   pl.BlockSpec((1, t, MLA_HEADS * HEAD_LANES), lambda bi, si: (bi, si, 0)),
                   pl.BlockSpec((1, ATTN_WIDTH, t), lambda bi, si: (bi, 0, si)),
                   pl.BlockSpec((1, t, CONV_WIDTH), lambda bi, si: (bi, si, 0))],
        out_shape=[jax.ShapeDtypeStruct((b, s, MLA_HEADS * HEAD_LANES), BF16),
                   jax.ShapeDtypeStruct((b, s, MLA_HEADS * HEAD_LANES), BF16),
                   jax.ShapeDtypeStruct((b, ATTN_WIDTH, s), BF16),
                   jax.ShapeDtypeStruct((b, s, CONV_WIDTH), BF16)],
        scratch_shapes=[pltpu.VMEM((8, CONV_WIDTH), F32)],
        compiler_params=pltpu.CompilerParams(dimension_semantics=("arbitrary", "arbitrary"),
                                             vmem_limit_bytes=56 * 1024 * 1024, flags=SCHED_FLAGS),
        name="mixer_proj",
    )(x, mod, tab, win, qg, wq, kvg, wk, wvt, cw, cb, og)


def _attn_kernel(q_ref, k_ref, vt_ref, g_ref, o_ref, s_ref, ot_ref):
    seq = q_ref.shape[1]
    qry_chunk = lax.broadcasted_iota(jnp.int32, (1, Q_TILE), 1) // CHUNK
    ones = jnp.ones((16, K_TILE), BF16)
    dn = (((1,), (1,)), ((), ()))

    def scores(qi, hh):
        q0 = qi * Q_TILE
        hl = slice(hh * HEAD_LANES, (hh + 1) * HEAD_LANES)
        q = q_ref[0, q0:q0 + Q_TILE, hl]
        if qi > 0:
            s_ref[hh, 0:q0, :] = lax.dot_general(k_ref[0, 0:q0, hl], q, dn, preferred_element_type=F32)
        for half in range(2):
            nk = (half + 1) * HALF
            sd = lax.dot_general(k_ref[0, q0:q0 + nk, hl], q[half * HALF:(half + 1) * HALF], dn,
                                 preferred_element_type=F32)
            cols = slice(half * HALF, (half + 1) * HALF)
            for kc in range(nk // CHUNK):
                rows = slice(kc * CHUNK, (kc + 1) * CHUNK)
                vis = qry_chunk[:, cols] >= kc
                s_ref[hh, q0 + kc * CHUNK:q0 + (kc + 1) * CHUNK, cols] = jnp.where(vis, sd[rows], NEG)

    def softmax_pv(qi, hh):
        kmax = (qi + 1) * Q_TILE
        n_full = kmax // K_TILE - 1
        last = slice(n_full * K_TILE, kmax)
        right = slice(HALF, Q_TILE)
        m = None
        for j in range(n_full):
            mj = jnp.max(s_ref[hh, j * K_TILE:(j + 1) * K_TILE, :], axis=0, keepdims=True)
            m = mj if m is None else jnp.maximum(m, mj)
        m_last = jnp.max(s_ref[hh, last, right], axis=0, keepdims=True)
        m = jnp.concatenate([m[:, :HALF], jnp.maximum(m[:, right], m_last)], axis=1)
        o = None
        for j in range(n_full):
            rows = slice(j * K_TILE, (j + 1) * K_TILE)
            p = jnp.exp2(s_ref[hh, rows, :] - m).astype(BF16)
            lhs = jnp.concatenate([vt_ref[0, hh * V_HEAD_DIM:(hh + 1) * V_HEAD_DIM, rows], ones], axis=0)
            oj = jnp.dot(lhs, p, preferred_element_type=F32)
            o = oj if o is None else o + oj
        p = jnp.exp2(s_ref[hh, last, right] - m[:, right]).astype(BF16)
        lhs = jnp.concatenate([vt_ref[0, hh * V_HEAD_DIM:(hh + 1) * V_HEAD_DIM, last], ones], axis=0)
        o = jnp.concatenate([o[:, :HALF], o[:, right] + jnp.dot(lhs, p, preferred_element_type=F32)], axis=1)
        o = o[0:V_HEAD_DIM] / o[V_HEAD_DIM:V_HEAD_DIM + 1]
        ms = jnp.mean(o * o, axis=0, keepdims=True)
        ot_ref[hh * V_HEAD_DIM:(hh + 1) * V_HEAD_DIM, :] = o * lax.rsqrt(ms + RMS_EPS)

    chains = [(qi, hh) for qi in range(seq // Q_TILE) for hh in range(2)]
    scores(*chains[0])
    for n, (qi, hh) in enumerate(chains):
        if n + 1 < len(chains):
            scores(*chains[n + 1])
        softmax_pv(qi, hh)
        if hh == 1:
            o_ref[0, qi * Q_TILE:(qi + 1) * Q_TILE, :] = (ot_ref[...].T * g_ref[...]).astype(BF16)


def _attention(q, k, vt, g):
    b, s, _ = q.shape
    pairs = MLA_HEADS // 2
    return pl.pallas_call(
        _attn_kernel,
        grid=(b, pairs),
        in_specs=[pl.BlockSpec((1, s, 2 * HEAD_LANES), lambda bi, pi: (bi, 0, pi)),
                  pl.BlockSpec((1, s, 2 * HEAD_LANES), lambda bi, pi: (bi, 0, pi)),
                  pl.BlockSpec((1, 2 * V_HEAD_DIM, s), lambda bi, pi: (bi, pi, 0)),
                  pl.BlockSpec((1, 2 * V_HEAD_DIM), lambda bi, pi: (0, pi))],
        out_specs=pl.BlockSpec((1, s, 2 * V_HEAD_DIM), lambda bi, pi: (bi, 0, pi)),
        out_shape=jax.ShapeDtypeStruct((b, s, ATTN_WIDTH), BF16),
        scratch_shapes=[pltpu.VMEM((2, s, Q_TILE), F32),
                        pltpu.VMEM((2 * V_HEAD_DIM, Q_TILE), F32)],
        compiler_params=pltpu.CompilerParams(dimension_semantics=("arbitrary",) * 2,
                                             vmem_limit_bytes=48 * 1024 * 1024, flags=SCHED_FLAGS),
        name="attention",
    )(q, k, vt, g)


def _ffn_kernel(x_ref, ya_ref, yc_ref, mod_ref, woa_ref, woc_ref, ln1g_ref, ln1b_ref, wup_ref,
                fcw_ref, fcb_ref, wdn_ref, ln2g_ref, ln2b_ref, o_ref, carry_ref, act_ref):
    t = TOK_TILE
    si = pl.program_id(1)
    gate_m = mod_ref[0, 2:3, :]
    shift_f = mod_ref[0, 3:4, :]
    scale_f = mod_ref[0, 4:5, :]
    gate_f = mod_ref[0, 5:6, :]

    @pl.when(si == 0)
    def _():
        carry_ref[...] = jnp.zeros(carry_ref.shape, F32)

    blocks = [(r, r + t // FFN_ROW_BLOCKS) for r in range(0, t, t // FFN_ROW_BLOCKS)]
    x1s, hs = [], []
    for r0, r1 in blocks:
        mix = (jnp.dot(ya_ref[0, r0:r1, :], woa_ref[...], preferred_element_type=F32)
               + jnp.dot(yc_ref[0, r0:r1, :], woc_ref[...], preferred_element_type=F32))
        x1 = _layer_norm(DEEPNORM_ALPHA * x_ref[0, r0:r1, :] + gate_m * mix) * ln1g_ref[...] + ln1b_ref[...]
        x1s.append(x1)
        hs.append((_layer_norm(x1) * (1.0 + scale_f) + shift_f).astype(BF16))

    def conv_cols(cols, h, prev, last):
        u = jnp.dot(h, wup_ref[:, cols], preferred_element_type=F32)
        cat = jnp.concatenate([prev, u], axis=0)
        u1 = pltpu.roll(cat, 1, axis=0)[8:]
        u2 = pltpu.roll(cat, 2, axis=0)[8:]
        tail = u[u.shape[0] - 8:]
        if last:
            carry_ref[:, cols] = tail
        y = fcb_ref[:, cols] + fcw_ref[0:1, cols] * u2 + fcw_ref[1:2, cols] * u1 + fcw_ref[2:3, cols] * u
        return y, tail

    for c0 in range(0, D_FF, FF_CHUNK):
        c1 = min(c0 + FF_CHUNK, D_FF)
        gcols = slice(c0, c1)
        vcols = slice(D_FF + c0, D_FF + c1)
        gprev, vprev = carry_ref[:, gcols], carry_ref[:, vcols]
        for bi, ((r0, r1), h) in enumerate(zip(blocks, hs)):
            last = bi == len(blocks) - 1
            g, gprev = conv_cols(gcols, h, gprev, last)
            v, vprev = conv_cols(vcols, h, vprev, last)
            act_ref[r0:r1, c0:c1] = (g * _sigmoid(g) * v).astype(BF16)

    for (r0, r1), x1 in zip(blocks, x1s):
        ff = jnp.dot(act_ref[r0:r1, :], wdn_ref[...], preferred_element_type=F32)
        o_ref[0, r0:r1, :] = (_layer_norm(DEEPNORM_ALPHA * x1 + gate_f * ff) * ln2g_ref[...]
                              + ln2b_ref[...])


def _out_ffn(x, ya, yc, mod, woa, woc, ln1g, ln1b, wup, fcw, fcb, wdn, ln2g, ln2b):
    b, s, d = x.shape
    t = TOK_TILE
    const = lambda shape: pl.BlockSpec(shape, lambda bi, si: (0,) * len(shape),
                                       pipeline_mode=pl.Buffered(1))
    tok = lambda width: pl.BlockSpec((1, t, width), lambda bi, si: (bi, si, 0))
    return pl.pallas_call(
        _ffn_kernel,
        grid=(b, s // t),
        in_specs=[tok(d), tok(ATTN_WIDTH), tok(CONV_WIDTH),
                  pl.BlockSpec((1, 6, d), lambda bi, si: (bi, 0, 0)),
                  const(woa.shape), const(woc.shape), const(ln1g.shape), const(ln1b.shape),
                  const(wup.shape), const(fcw.shape), const(fcb.shape), const(wdn.shape),
                  const(ln2g.shape), const(ln2b.shape)],
        out_specs=tok(d),
        out_shape=jax.ShapeDtypeStruct((b, s, d), F32),
        scratch_shapes=[pltpu.VMEM((8, 2 * D_FF), F32),
                        pltpu.VMEM((t, D_FF), BF16)],
        compiler_params=pltpu.CompilerParams(dimension_semantics=("arbitrary", "arbitrary"),
                                             vmem_limit_bytes=56 * 1024 * 1024, flags=SCHED_FLAGS),
        name="out_ffn",
    )(x, ya, yc, mod, woa, woc, ln1g, ln1b, wup, fcw, fcb, wdn, ln2g, ln2b)


def _swap_halves(w):
    half = w.shape[-1] // 2
    return jnp.concatenate([w[..., half:], w[..., :half]], axis=-1)


def _prep_in_proj(w_in):
    c0 = Q_LORA_RANK + KV_LORA_RANK
    kr = w_in[:, c0:c0 + QK_ROPE_DIM]
    krs = _swap_halves(kr)
    return jnp.concatenate([w_in[:, :c0], kr, krs, kr, krs, w_in[:, c0 + QK_ROPE_DIM:]], axis=1).astype(BF16)


def _prep_q_up(w_q_up):
    w = w_q_up.reshape(Q_LORA_RANK, MLA_HEADS, QK_NOPE_DIM + QK_ROPE_DIM)
    rope = w[..., QK_NOPE_DIM:]
    w = jnp.concatenate([w[..., :QK_NOPE_DIM], rope, _swap_halves(rope)], axis=-1)
    return w.reshape(Q_LORA_RANK, MLA_HEADS * HEAD_LANES).astype(BF16)


def _prep_kv_up(w_kv_up):
    w = w_kv_up.reshape(KV_LORA_RANK, MLA_HEADS, QK_NOPE_DIM + V_HEAD_DIM)
    wk = w[..., :QK_NOPE_DIM].reshape(KV_LORA_RANK, MLA_HEADS * QK_NOPE_DIM)
    wvt = w[..., QK_NOPE_DIM:].reshape(KV_LORA_RANK, ATTN_WIDTH).T
    return wk.astype(BF16), wvt.astype(BF16)


def kernel(x, c, positions, w_ada, b_ada, w_in, q_norm_g, w_q_up, kv_norm_g, w_kv_up, conv_w, conv_b,
           out_norm_g, w_out, ln1_g, ln1_b, w_up, ffn_conv_w, ffn_conv_b, w_down, ln2_g, ln2_b):
    b, s, d = x.shape
    depth = w_ada.shape[0]
    tab = _rope_table(positions)
    for l in range(depth):
        mod = _adaln_mod(c, w_ada[l], b_ada[l]).reshape(b, 6, d)
        wk, wvt = _prep_kv_up(w_kv_up[l])
        q, k, vt, yc = _mixer_proj(
            x, mod, tab, _prep_in_proj(w_in[l]), q_norm_g[l].reshape(1, -1), _prep_q_up(w_q_up[l]),
            kv_norm_g[l].reshape(1, -1), wk, wvt, conv_w[l], conv_b[l].reshape(1, -1),
            out_norm_g[l, ATTN_WIDTH:].reshape(1, -1))
        ya = _attention(q, k, vt, out_norm_g[l, :ATTN_WIDTH].reshape(1, -1))
        x = _out_ffn(
            x, ya, yc, mod, w_out[l, :ATTN_WIDTH].astype(BF16), w_out[l, ATTN_WIDTH:].astype(BF16),
            ln1_g[l].reshape(1, -1), ln1_b[l].reshape(1, -1), w_up[l].astype(BF16),
            ffn_conv_w[l], ffn_conv_b[l].reshape(1, -1),
            w_down[l].astype(BF16), ln2_g[l].reshape(1, -1), ln2_b[l].reshape(1, -1))
    return x
```

```python
import jax
import jax.numpy as jnp
from jax import lax
from jax.experimental import pallas as pl
from jax.experimental.pallas import tpu as pltpu

D_MODEL = 1024
SEQ = 2048
CHUNK = 64
MLA_HEADS = 8
QK_NOPE_DIM = 64
QK_ROPE_DIM = 32
V_HEAD_DIM = 64
Q_LORA_RANK = 256
KV_LORA_RANK = 128
ATTN_WIDTH = MLA_HEADS * V_HEAD_DIM
CONV_WIDTH = D_MODEL - ATTN_WIDTH
CONV_GROUPS = 8
D_FF = 2816
ROPE_THETA = 10000.0
RMS_EPS = 1e-6
LN_EPS = 1e-5
DEEPNORM_ALPHA = 2.0 ** 0.25

LANES = 128
HEAD_LANES = 128
IN_EXT = Q_LORA_RANK + KV_LORA_RANK + LANES + 3 * CONV_WIDTH
TOK_TILE = 512
PROJ_TILE = 2048
Q_TILE = 512
K_TILE = 256
HALF = Q_TILE // 2
assert K_TILE == HALF
FF_CHUNK = 256
N_FF_CHUNKS = D_FF // FF_CHUNK
FFN_ROW_BLOCKS = 2
PROJ_ROW_BLOCKS = 8
ADALN_K_BLOCK = 256
NEG = -1e30
Q_SCALE = (QK_NOPE_DIM + QK_ROPE_DIM) ** -0.5 * 1.4426950408889634

F32 = jnp.float32
BF16 = jnp.bfloat16
SCHED_FLAGS = None


def _layer_norm(x):
    mu = jnp.mean(x, axis=-1, keepdims=True)
    xc = x - mu
    var = jnp.mean(xc * xc, axis=-1, keepdims=True)
    return xc * lax.rsqrt(var + LN_EPS)


def _rms(x):
    return x * lax.rsqrt(jnp.mean(x * x, axis=-1, keepdims=True) + RMS_EPS)


def _sigmoid(x):
    return 1.0 / (1.0 + jnp.exp(-x))


def _split3(x):
    hi = x.astype(BF16)
    r1 = x - hi.astype(F32)
    mid = r1.astype(BF16)
    lo = (r1 - mid.astype(F32)).astype(BF16)
    return hi, mid, lo


def _rope_kernel(pos_ref, freq_ref, e_ref, tab_ref):
    ang = freq_ref[...] * pos_ref[0].astype(F32)
    pieces = jnp.concatenate(_split3(jnp.cos(ang)) + _split3(jnp.sin(ang)), axis=0)
    tab_ref[0] = lax.dot_general(pieces, e_ref[...], (((0,), (0,)), ((), ())),
                                 preferred_element_type=F32)


def _rope_expansion():
    half = QK_ROPE_DIM // 2
    src = jnp.arange(6 * half)[:, None]
    dst = jnp.arange(LANES)[None, :]
    same_freq = src % half == dst % half
    is_sin_row = src >= 3 * half
    group = dst // half % 4
    e_cos = jnp.where(same_freq & ~is_sin_row & (group < 2), 1.0, 0.0)
    e_sin = jnp.where(same_freq & is_sin_row & (group == 2), -1.0,
                      jnp.where(same_freq & is_sin_row & (group == 3), 1.0, 0.0))
    return (e_cos + e_sin).astype(BF16)


def _rope_table(positions):
    b, s = positions.shape
    half = QK_ROPE_DIM // 2
    inv_freq = ROPE_THETA ** (-jnp.arange(0, QK_ROPE_DIM, 2, dtype=F32) / QK_ROPE_DIM)
    e = _rope_expansion()
    return pl.pallas_call(
        _rope_kernel,
        grid=(b,),
        in_specs=[pl.BlockSpec((1, 1, s), lambda i: (i, 0, 0)),
                  pl.BlockSpec((half, 1), lambda i: (0, 0)),
                  pl.BlockSpec(e.shape, lambda i: (0, 0))],
        out_specs=pl.BlockSpec((1, s, LANES), lambda i: (i, 0, 0)),
        out_shape=jax.ShapeDtypeStruct((b, s, LANES), F32),
        compiler_params=pltpu.CompilerParams(dimension_semantics=("arbitrary",),
                                             vmem_limit_bytes=32 * 1024 * 1024),
        name="rope_table",
    )(positions.reshape(b, 1, s), inv_freq.reshape(half, 1), e)


def _mod_kernel(c_ref, w_ref, b_ref, o_ref):
    @pl.when(pl.program_id(0) == 0)
    def _():
        o_ref[...] = jnp.broadcast_to(b_ref[...], o_ref.shape)

    c = c_ref[...]
    act = c * _sigmoid(c)
    o_ref[...] += jnp.dot(act, w_ref[...], preferred_element_type=F32, precision=lax.Precision.HIGHEST)


def _adaln_mod(c, w_ada, b_ada):
    b, d = c.shape
    n = w_ada.shape[1]
    kb = ADALN_K_BLOCK
    return pl.pallas_call(
        _mod_kernel,
        grid=(d // kb,),
        in_specs=[pl.BlockSpec((b, kb), lambda j: (0, j)),
                  pl.BlockSpec((kb, n), lambda j: (j, 0)),
                  pl.BlockSpec((1, n), lambda j: (0, 0))],
        out_specs=pl.BlockSpec((b, n), lambda j: (0, 0)),
        out_shape=jax.ShapeDtypeStruct((b, n), F32),
        compiler_params=pltpu.CompilerParams(dimension_semantics=("arbitrary",),
                                             vmem_limit_bytes=32 * 1024 * 1024),
        name="adaln_mod",
    )(c, w_ada, b_ada.reshape(1, n))


def _proj_kernel(x_ref, mod_ref, tab_ref, win_ref, qg_ref, wq_ref, kvg_ref, wk_ref, wvt_ref,
                 cw_ref, cb_ref, og_ref, q_ref, k_ref, vt_ref, yc_ref, carry_ref):
    t = PROJ_TILE
    si = pl.program_id(1)
    shift = mod_ref[0, 0:1, :]
    scale = mod_ref[0, 1:2, :]
    lane = lax.broadcasted_iota(jnp.int32, (1, LANES), 1)
    is_nope = lane < QK_NOPE_DIM
    lo = lane < (CONV_WIDTH // CONV_GROUPS)
    c0 = Q_LORA_RANK
    c1 = c0 + KV_LORA_RANK
    c2 = c1 + LANES

    @pl.when(si == 0)
    def _():
        carry_ref[...] = jnp.zeros((8, CONV_WIDTH), F32)

    prev = carry_ref[...]
    for r0 in range(0, t, t // PROJ_ROW_BLOCKS):
        r1 = r0 + t // PROJ_ROW_BLOCKS
        h = (_layer_norm(x_ref[0, r0:r1, :]) * (1.0 + scale) + shift).astype(BF16)
        proj = jnp.dot(h, win_ref[...], preferred_element_type=F32)
        tab = tab_ref[0, r0:r1, :]

        cq = (_rms(proj[:, 0:Q_LORA_RANK]) * qg_ref[...]).astype(BF16)
        q = jnp.dot(cq, wq_ref[...], preferred_element_type=F32)
        q_tab = jnp.where(is_nope, Q_SCALE, Q_SCALE * tab)
        for hh in range(MLA_HEADS):
            sl = slice(hh * HEAD_LANES, (hh + 1) * HEAD_LANES)
            q_ref[0, r0:r1, sl] = (q[:, sl] * q_tab).astype(BF16)

        ckv = (_rms(proj[:, c0:c0 + KV_LORA_RANK]) * kvg_ref[...]).astype(BF16)
        knope = jnp.dot(ckv, wk_ref[...], preferred_element_type=F32)
        r = proj[:, c1:c1 + LANES] * tab
        kf = r + pltpu.roll(r, QK_ROPE_DIM, axis=1)
        for hp in range(MLA_HEADS // 2):
            pair = knope[:, hp * LANES:(hp + 1) * LANES]
            for hh, nope in ((2 * hp, pair), (2 * hp + 1, pltpu.roll(pair, QK_NOPE_DIM, axis=1))):
                sl = slice(hh * HEAD_LANES, (hh + 1) * HEAD_LANES)
                k_ref[0, r0:r1, sl] = jnp.where(is_nope, nope, kf).astype(BF16)
        vt_ref[0, :, r0:r1] = lax.dot_general(wvt_ref[...], ckv, (((1,), (1,)), ((), ())),
                                              preferred_element_type=F32).astype(BF16)

        gate_b = proj[:, c2:c2 + CONV_WIDTH]
        u = proj[:, c2 + CONV_WIDTH:c2 + 2 * CONV_WIDTH] * proj[:, c2 + 2 * CONV_WIDTH:c2 + 3 * CONV_WIDTH]
        cat = jnp.concatenate([prev, u], axis=0)
        u1 = pltpu.roll(cat, 1, axis=0)[8:]
        u2 = pltpu.roll(cat, 2, axis=0)[8:]
        prev = u[r1 - r0 - 8:]
        if r1 == t:
            carry_ref[...] = prev
        y = cb_ref[...] + cw_ref[0:1, :] * u2 + cw_ref[1:2, :] * u1 + cw_ref[2:3, :] * u
        y = gate_b * y
        for cc in range(CONV_WIDTH // LANES):
            sl = slice(cc * LANES, (cc + 1) * LANES)
            yv = y[:, sl]
            sq = yv * yv
            ms_lo = jnp.sum(jnp.where(lo, sq, 0.0), axis=-1, keepdims=True)
            ms_hi = jnp.sum(jnp.where(lo, 0.0, sq), axis=-1, keepdims=True)
            ms = jnp.where(lo, ms_lo, ms_hi) * (1.0 / (CONV_WIDTH // CONV_GROUPS))
            yc_ref[0, r0:r1, sl] = (yv * lax.rsqrt(ms + RMS_EPS) * og_ref[:, sl]).astype(BF16)


def _mixer_proj(x, mod, tab, win, qg, wq, kvg, wk, wvt, cw, cb, og):
    b, s, d = x.shape
    t = PROJ_TILE
    nt = s // t
    const = lambda shape: pl.BlockSpec(shape, lambda bi, si: (0,) * len(shape))
    return pl.pallas_call(
        _proj_kernel,
        grid=(b, nt),
        in_specs=[pl.BlockSpec((1, t, d), lambda bi, si: (bi, si, 0)),
                  pl.BlockSpec((1, 6, d), lambda bi, si: (bi, 0, 0)),
                  pl.BlockSpec((1, t, LANES), lambda bi, si: (bi, si, 0)),
                  const(win.shape), const(qg.shape), const(wq.shape), const(kvg.shape),
                  const(wk.shape), const(wvt.shape), const(cw.shape), const(cb.shape), const(og.shape)],
        out_specs=[pl.BlockSpec((1, t, MLA_HEADS * HEAD_LANES), lambda bi, si: (bi, si, 0)),
                   pl.BlockSpec((1, t, MLA_HEADS * HEAD_LANES), lambda bi, si: (bi, si, 0)),
                   pl.BlockSpec((1, ATTN_WIDTH, t), lambda bi, si: (bi, 0, si)),
                   pl.BlockSpec((1, t, CONV_WIDTH), lambda bi, si: (bi, si, 0))],
        out_shape=[jax.ShapeDtypeStruct((b, s, MLA_HEADS * HEAD_LANES), BF16),
                   jax.ShapeDtypeStruct((b, s, MLA_HEADS * HEAD_LANES), BF16),
                   jax.ShapeDtypeStruct((b, ATTN_WIDTH, s), BF16),
                   jax.ShapeDtypeStruct((b, s, CONV_WIDTH), BF16)],
        scratch_shapes=[pltpu.VMEM((8, CONV_WIDTH), F32)],
        compiler_params=pltpu.CompilerParams(dimension_semantics=("arbitrary", "arbitrary"),
                                             vmem_limit_bytes=56 * 1024 * 1024, flags=SCHED_FLAGS),
        name="mixer_proj",
    )(x, mod, tab, win, qg, wq, kvg, wk, wvt, cw, cb, og)


def _attn_kernel(q_ref, k_ref, vt_ref, g_ref, o_ref, s_ref, ot_ref):
    seq = q_ref.shape[1]
    qry_chunk = lax.broadcasted_iota(jnp.int32, (1, Q_TILE), 1) // CHUNK
    ones = jnp.ones((16, K_TILE), BF16)
    dn = (((1,), (1,)), ((), ()))

    def scores(qi, hh):
        q0 = qi * Q_TILE
        hl = slice(hh * HEAD_LANES, (hh + 1) * HEAD_LANES)
        q = q_ref[0, q0:q0 + Q_TILE, hl]
        if qi > 0:
            s_ref[hh, 0:q0, :] = lax.dot_general(k_ref[0, 0:q0, hl], q, dn, preferred_element_type=F32)
        for half in range(2):
            nk = (half + 1) * HALF
            sd = lax.dot_general(k_ref[0, q0:q0 + nk, hl], q[half * HALF:(half + 1) * HALF], dn,
                                 preferred_element_type=F32)
            cols = slice(half * HALF, (half + 1) * HALF)
            for kc in range(nk // CHUNK):
                rows = slice(kc * CHUNK, (kc + 1) * CHUNK)
                vis = qry_chunk[:, cols] >= kc
                s_ref[hh, q0 + kc * CHUNK:q0 + (kc + 1) * CHUNK, cols] = jnp.where(vis, sd[rows], NEG)

    def softmax_pv(qi, hh):
        kmax = (qi + 1) * Q_TILE
        n_full = kmax // K_TILE - 1
        last = slice(n_full * K_TILE, kmax)
        right = slice(HALF, Q_TILE)
        m = None
        for j in range(n_full):
            mj = jnp.max(s_ref[hh, j * K_TILE:(j + 1) * K_TILE, :], axis=0, keepdims=True)
            m = mj if m is None else jnp.maximum(m, mj)
        m_last = jnp.max(s_ref[hh, last, right], axis=0, keepdims=True)
        m = jnp.concatenate([m[:, :HALF], jnp.maximum(m[:, right], m_last)], axis=1)
        o = None
        for j in range(n_full):
            rows = slice(j * K_TILE, (j + 1) * K_TILE)
            p = jnp.exp2(s_ref[hh, rows, :] - m).astype(BF16)
            lhs = jnp.concatenate([vt_ref[0, hh * V_HEAD_DIM:(hh + 1) * V_HEAD_DIM, rows], ones], axis=0)
            oj = jnp.dot(lhs, p, preferred_element_type=F32)
            o = oj if o is None else o + oj
        p = jnp.exp2(s_ref[hh, last, right] - m[:, right]).astype(BF16)
        lhs = jnp.concatenate([vt_ref[0, hh * V_HEAD_DIM:(hh + 1) * V_HEAD_DIM, last], ones], axis=0)
        o = jnp.concatenate([o[:, :HALF], o[:, right] + jnp.dot(lhs, p, preferred_element_type=F32)], axis=1)
        o = o[0:V_HEAD_DIM] / o[V_HEAD_DIM:V_HEAD_DIM + 1]
        ms = jnp.mean(o * o, axis=0, keepdims=True)
        ot_ref[hh * V_HEAD_DIM:(hh + 1) * V_HEAD_DIM, :] = o * lax.rsqrt(ms + RMS_EPS)

    chains = [(qi, hh) for qi in range(seq // Q_TILE) for hh in range(2)]
    scores(*chains[0])
    for n, (qi, hh) in enumerate(chains):
        if n + 1 < len(chains):
            scores(*chains[n + 1])
        softmax_pv(qi, hh)
        if hh == 1:
            o_ref[0, qi * Q_TILE:(qi + 1) * Q_TILE, :] = (ot_ref[...].T * g_ref[...]).astype(BF16)


def _attention(q, k, vt, g):
    b, s, _ = q.shape
    pairs = MLA_HEADS // 2
    return pl.pallas_call(
        _attn_kernel,
        grid=(b, pairs),
        in_specs=[pl.BlockSpec((1, s, 2 * HEAD_LANES), lambda bi, pi: (bi, 0, pi)),
                  pl.BlockSpec((1, s, 2 * HEAD_LANES), lambda bi, pi: (bi, 0, pi)),
                  pl.BlockSpec((1, 2 * V_HEAD_DIM, s), lambda bi, pi: (bi, pi, 0)),
                  pl.BlockSpec((1, 2 * V_HEAD_DIM), lambda bi, pi: (0, pi))],
        out_specs=pl.BlockSpec((1, s, 2 * V_HEAD_DIM), lambda bi, pi: (bi, 0, pi)),
        out_shape=jax.ShapeDtypeStruct((b, s, ATTN_WIDTH), BF16),
        scratch_shapes=[pltpu.VMEM((2, s, Q_TILE), F32),
                        pltpu.VMEM((2 * V_HEAD_DIM, Q_TILE), F32)],
        compiler_params=pltpu.CompilerParams(dimension_semantics=("arbitrary",) * 2,
                                             vmem_limit_bytes=48 * 1024 * 1024, flags=SCHED_FLAGS),
        name="attention",
    )(q, k, vt, g)


def _ffn_kernel(x_ref, ya_ref, yc_ref, mod_ref, woa_ref, woc_ref, ln1g_ref, ln1b_ref, wup_ref,
                fcw_ref, fcb_ref, wdn_ref, ln2g_ref, ln2b_ref, o_ref, carry_ref, act_ref):
    t = TOK_TILE
    si = pl.program_id(1)
    gate_m = mod_ref[0, 2:3, :]
    shift_f = mod_ref[0, 3:4, :]
    scale_f = mod_ref[0, 4:5, :]
    gate_f = mod_ref[0, 5:6, :]

    @pl.when(si == 0)
    def _():
        carry_ref[...] = jnp.zeros(carry_ref.shape, F32)

    blocks = [(r, r + t // FFN_ROW_BLOCKS) for r in range(0, t, t // FFN_ROW_BLOCKS)]
    x1s, hs = [], []
    for r0, r1 in blocks:
        mix = (jnp.dot(ya_ref[0, r0:r1, :], woa_ref[...], preferred_element_type=F32)
               + jnp.dot(yc_ref[0, r0:r1, :], woc_ref[...], preferred_element_type=F32))
        x1 = _layer_norm(DEEPNORM_ALPHA * x_ref[0, r0:r1, :] + gate_m * mix) * ln1g_ref[...] + ln1b_ref[...]
        x1s.append(x1)
        hs.append((_layer_norm(x1) * (1.0 + scale_f) + shift_f).astype(BF16))

    def conv_cols(cols, h, prev, last):
        u = jnp.dot(h, wup_ref[:, cols], preferred_element_type=F32)
        cat = jnp.concatenate([prev, u], axis=0)
        u1 = pltpu.roll(cat, 1, axis=0)[8:]
        u2 = pltpu.roll(cat, 2, axis=0)[8:]
        tail = u[u.shape[0] - 8:]
        if last:
            carry_ref[:, cols] = tail
        y = fcb_ref[:, cols] + fcw_ref[0:1, cols] * u2 + fcw_ref[1:2, cols] * u1 + fcw_ref[2:3, cols] * u
        return y, tail

    for c in range(N_FF_CHUNKS):
        gcols = slice(c * FF_CHUNK, (c + 1) * FF_CHUNK)
        vcols = slice(D_FF + c * FF_CHUNK, D_FF + (c + 1) * FF_CHUNK)
        gprev, vprev = carry_ref[:, gcols], carry_ref[:, vcols]
        for bi, ((r0, r1), h) in enumerate(zip(blocks, hs)):
            last = bi == len(blocks) - 1
            g, gprev = conv_cols(gcols, h, gprev, last)
            v, vprev = conv_cols(vcols, h, vprev, last)
            act_ref[r0:r1, c * FF_CHUNK:(c + 1) * FF_CHUNK] = (g * _sigmoid(g) * v).astype(BF16)

    for (r0, r1), x1 in zip(blocks, x1s):
        ff = jnp.dot(act_ref[r0:r1, :], wdn_ref[...], preferred_element_type=F32)
        o_ref[0, r0:r1, :] = (_layer_norm(DEEPNORM_ALPHA * x1 + gate_f * ff) * ln2g_ref[...]
                              + ln2b_ref[...])


def _out_ffn(x, ya, yc, mod, woa, woc, ln1g, ln1b, wup, fcw, fcb, wdn, ln2g, ln2b):
    b, s, d = x.shape
    t = TOK_TILE
    const = lambda shape: pl.BlockSpec(shape, lambda bi, si: (0,) * len(shape),
                                       pipeline_mode=pl.Buffered(1))
    tok = lambda width: pl.BlockSpec((1, t, width), lambda bi, si: (bi, si, 0))
    return pl.pallas_call(
        _ffn_kernel,
        grid=(b, s // t),
        in_specs=[tok(d), tok(ATTN_WIDTH), tok(CONV_WIDTH),
                  pl.BlockSpec((1, 6, d), lambda bi, si: (bi, 0, 0)),
                  const(woa.shape), const(woc.shape), const(ln1g.shape), const(ln1b.shape),
                  const(wup.shape), const(fcw.shape), const(fcb.shape), const(wdn.shape),
                  const(ln2g.shape), const(ln2b.shape)],
        out_specs=tok(d),
        out_shape=jax.ShapeDtypeStruct((b, s, d), F32),
        scratch_shapes=[pltpu.VMEM((8, 2 * D_FF), F32),
                        pltpu.VMEM((t, D_FF), BF16)],
        compiler_params=pltpu.CompilerParams(dimension_semantics=("arbitrary", "arbitrary"),
                                             vmem_limit_bytes=56 * 1024 * 1024, flags=SCHED_FLAGS),
        name="out_ffn",
    )(x, ya, yc, mod, woa, woc, ln1g, ln1b, wup, fcw, fcb, wdn, ln2g, ln2b)


def _swap_halves(w):
    half = w.shape[-1] // 2
    return jnp.concatenate([w[..., half:], w[..., :half]], axis=-1)


def _prep_in_proj(w_in):
    c0 = Q_LORA_RANK + KV_LORA_RANK
    kr = w_in[:, c0:c0 + QK_ROPE_DIM]
    krs = _swap_halves(kr)
    return jnp.concatenate([w_in[:, :c0], kr, krs, kr, krs, w_in[:, c0 + QK_ROPE_DIM:]], axis=1).astype(BF16)


def _prep_q_up(w_q_up):
    w = w_q_up.reshape(Q_LORA_RANK, MLA_HEADS, QK_NOPE_DIM + QK_ROPE_DIM)
    rope = w[..., QK_NOPE_DIM:]
    w = jnp.concatenate([w[..., :QK_NOPE_DIM], rope, _swap_halves(rope)], axis=-1)
    return w.reshape(Q_LORA_RANK, MLA_HEADS * HEAD_LANES).astype(BF16)


def _prep_kv_up(w_kv_up):
    w = w_kv_up.reshape(KV_LORA_RANK, MLA_HEADS, QK_NOPE_DIM + V_HEAD_DIM)
    wk = w[..., :QK_NOPE_DIM].reshape(KV_LORA_RANK, MLA_HEADS * QK_NOPE_DIM)
    wvt = w[..., QK_NOPE_DIM:].reshape(KV_LORA_RANK, ATTN_WIDTH).T
    return wk.astype(BF16), wvt.astype(BF16)


def kernel(x, c, positions, w_ada, b_ada, w_in, q_norm_g, w_q_up, kv_norm_g, w_kv_up, conv_w, conv_b,
           out_norm_g, w_out, ln1_g, ln1_b, w_up, ffn_conv_w, ffn_conv_b, w_down, ln2_g, ln2_b):
    b, s, d = x.shape
    depth = w_ada.shape[0]
    tab = _rope_table(positions)
    for l in range(depth):
        mod = _adaln_mod(c, w_ada[l], b_ada[l]).reshape(b, 6, d)
        wk, wvt = _prep_kv_up(w_kv_up[l])
        q, k, vt, yc = _mixer_proj(
            x, mod, tab, _prep_in_proj(w_in[l]), q_norm_g[l].reshape(1, -1), _prep_q_up(w_q_up[l]),
            kv_norm_g[l].reshape(1, -1), wk, wvt, conv_w[l], conv_b[l].reshape(1, -1),
            out_norm_g[l, ATTN_WIDTH:].reshape(1, -1))
        ya = _attention(q, k, vt, out_norm_g[l, :ATTN_WIDTH].reshape(1, -1))
        x = _out_ffn(
            x, ya, yc, mod, w_out[l, :ATTN_WIDTH].astype(BF16), w_out[l, ATTN_WIDTH:].astype(BF16),
            ln1_g[l].reshape(1, -1), ln1_b[l].reshape(1, -1), w_up[l].astype(BF16),
            ffn_conv_w[l], ffn_conv_b[l].reshape(1, -1),
            w_down[l].astype(BF16), ln2_g[l].reshape(1, -1), ln2_b[l].reshape(1, -1))
    return x
```

```python
import jax
import jax.numpy as jnp
from jax import lax
from jax.experimental import pallas as pl
from jax.experimental.pallas import tpu as pltpu

D_MODEL = 1024
SEQ = 2048
CHUNK = 64
MLA_HEADS = 8
QK_NOPE_DIM = 64
QK_ROPE_DIM = 32
V_HEAD_DIM = 64
Q_LORA_RANK = 256
KV_LORA_RANK = 128
ATTN_WIDTH = MLA_HEADS * V_HEAD_DIM
CONV_WIDTH = D_MODEL - ATTN_WIDTH
CONV_GROUPS = 8
D_FF = 2816
ROPE_THETA = 10000.0
RMS_EPS = 1e-6
LN_EPS = 1e-5
DEEPNORM_ALPHA = 2.0 ** 0.25

LANES = 128
HEAD_LANES = 128
IN_EXT = Q_LORA_RANK + KV_LORA_RANK + LANES + 3 * CONV_WIDTH
TOK_TILE = 512
PROJ_TILE = 1024
Q_TILE = 512
K_TILE = 256
HALF = Q_TILE // 2
assert K_TILE == HALF
FF_CHUNK = 256
N_FF_CHUNKS = D_FF // FF_CHUNK
FFN_ROW_BLOCKS = 2
PROJ_ROW_BLOCKS = 4
ADALN_K_BLOCK = 256
NEG = -1e30
Q_SCALE = (QK_NOPE_DIM + QK_ROPE_DIM) ** -0.5 * 1.4426950408889634

F32 = jnp.float32
BF16 = jnp.bfloat16
SCHED_FLAGS = None


def _layer_norm(x):
    mu = jnp.mean(x, axis=-1, keepdims=True)
    xc = x - mu
    var = jnp.mean(xc * xc, axis=-1, keepdims=True)
    return xc * lax.rsqrt(var + LN_EPS)


def _rms(x):
    return x * lax.rsqrt(jnp.mean(x * x, axis=-1, keepdims=True) + RMS_EPS)


def _sigmoid(x):
    return 1.0 / (1.0 + jnp.exp(-x))


def _split3(x):
    hi = x.astype(BF16)
    r1 = x - hi.astype(F32)
    mid = r1.astype(BF16)
    lo = (r1 - mid.astype(F32)).astype(BF16)
    return hi, mid, lo


def _rope_kernel(pos_ref, freq_ref, e_ref, tab_ref):
    ang = freq_ref[...] * pos_ref[0].astype(F32)
    pieces = jnp.concatenate(_split3(jnp.cos(ang)) + _split3(jnp.sin(ang)), axis=0)
    tab_ref[0] = lax.dot_general(pieces, e_ref[...], (((0,), (0,)), ((), ())),
                                 preferred_element_type=F32)


def _rope_expansion():
    half = QK_ROPE_DIM // 2
    src = jnp.arange(6 * half)[:, None]
    dst = jnp.arange(LANES)[None, :]
    same_freq = src % half == dst % half
    is_sin_row = src >= 3 * half
    group = dst // half % 4
    e_cos = jnp.where(same_freq & ~is_sin_row & (group < 2), 1.0, 0.0)
    e_sin = jnp.where(same_freq & is_sin_row & (group == 2), -1.0,
                      jnp.where(same_freq & is_sin_row & (group == 3), 1.0, 0.0))
    return (e_cos + e_sin).astype(BF16)


def _rope_table(positions):
    b, s = positions.shape
    half = QK_ROPE_DIM // 2
    inv_freq = ROPE_THETA ** (-jnp.arange(0, QK_ROPE_DIM, 2, dtype=F32) / QK_ROPE_DIM)
    e = _rope_expansion()
    return pl.pallas_call(
        _rope_kernel,
        grid=(b,),
        in_specs=[pl.BlockSpec((1, 1, s), lambda i: (i, 0, 0)),
                  pl.BlockSpec((half, 1), lambda i: (0, 0)),
                  pl.BlockSpec(e.shape, lambda i: (0, 0))],
        out_specs=pl.BlockSpec((1, s, LANES), lambda i: (i, 0, 0)),
        out_shape=jax.ShapeDtypeStruct((b, s, LANES), F32),
        compiler_params=pltpu.CompilerParams(dimension_semantics=("arbitrary",),
                                             vmem_limit_bytes=32 * 1024 * 1024),
        name="rope_table",
    )(positions.reshape(b, 1, s), inv_freq.reshape(half, 1), e)


def _mod_kernel(c_ref, w_ref, b_ref, o_ref):
    @pl.when(pl.program_id(0) == 0)
    def _():
        o_ref[...] = jnp.broadcast_to(b_ref[...], o_ref.shape)

    c = c_ref[...]
    act = c * _sigmoid(c)
    o_ref[...] += jnp.dot(act, w_ref[...], preferred_element_type=F32, precision=lax.Precision.HIGHEST)


def _adaln_mod(c, w_ada, b_ada):
    b, d = c.shape
    n = w_ada.shape[1]
    kb = ADALN_K_BLOCK
    return pl.pallas_call(
        _mod_kernel,
        grid=(d // kb,),
        in_specs=[pl.BlockSpec((b, kb), lambda j: (0, j)),
                  pl.BlockSpec((kb, n), lambda j: (j, 0)),
                  pl.BlockSpec((1, n), lambda j: (0, 0))],
        out_specs=pl.BlockSpec((b, n), lambda j: (0, 0)),
        out_shape=jax.ShapeDtypeStruct((b, n), F32),
        compiler_params=pltpu.CompilerParams(dimension_semantics=("arbitrary",),
                                             vmem_limit_bytes=32 * 1024 * 1024),
        name="adaln_mod",
    )(c, w_ada, b_ada.reshape(1, n))


def _proj_kernel(x_ref, mod_ref, tab_ref, win_ref, qg_ref, wq_ref, kvg_ref, wk_ref, wvt_ref,
                 cw_ref, cb_ref, og_ref, q_ref, k_ref, vt_ref, yc_ref, carry_ref):
    t = PROJ_TILE
    si = pl.program_id(1)
    shift = mod_ref[0, 0:1, :]
    scale = mod_ref[0, 1:2, :]
    lane = lax.broadcasted_iota(jnp.int32, (1, LANES), 1)
    is_nope = lane < QK_NOPE_DIM
    lo = lane < (CONV_WIDTH // CONV_GROUPS)
    c0 = Q_LORA_RANK
    c1 = c0 + KV_LORA_RANK
    c2 = c1 + LANES

    @pl.when(si == 0)
    def _():
        carry_ref[...] = jnp.zeros((8, CONV_WIDTH), F32)

    prev = carry_ref[...]
    for r0 in range(0, t, t // PROJ_ROW_BLOCKS):
        r1 = r0 + t // PROJ_ROW_BLOCKS
        h = (_layer_norm(x_ref[0, r0:r1, :]) * (1.0 + scale) + shift).astype(BF16)
        proj = jnp.dot(h, win_ref[...], preferred_element_type=F32)
        tab = tab_ref[0, r0:r1, :]

        cq = (_rms(proj[:, 0:Q_LORA_RANK]) * qg_ref[...]).astype(BF16)
        q = jnp.dot(cq, wq_ref[...], preferred_element_type=F32)
        q_tab = jnp.where(is_nope, Q_SCALE, Q_SCALE * tab)
        for hh in range(MLA_HEADS):
            sl = slice(hh * HEAD_LANES, (hh + 1) * HEAD_LANES)
            q_ref[0, r0:r1, sl] = (q[:, sl] * q_tab).astype(BF16)

        ckv = (_rms(proj[:, c0:c0 + KV_LORA_RANK]) * kvg_ref[...]).astype(BF16)
        knope = jnp.dot(ckv, wk_ref[...], preferred_element_type=F32)
        r = proj[:, c1:c1 + LANES] * tab
        kf = r + pltpu.roll(r, QK_ROPE_DIM, axis=1)
        for hp in range(MLA_HEADS // 2):
            pair = knope[:, hp * LANES:(hp + 1) * LANES]
            for hh, nope in ((2 * hp, pair), (2 * hp + 1, pltpu.roll(pair, QK_NOPE_DIM, axis=1))):
                sl = slice(hh * HEAD_LANES, (hh + 1) * HEAD_LANES)
                k_ref[0, r0:r1, sl] = jnp.where(is_nope, nope, kf).astype(BF16)
        vt_ref[0, :, r0:r1] = lax.dot_general(wvt_ref[...], ckv, (((1,), (1,)), ((), ())),
                                              preferred_element_type=F32).astype(BF16)

        gate_b = proj[:, c2:c2 + CONV_WIDTH]
        u = proj[:, c2 + CONV_WIDTH:c2 + 2 * CONV_WIDTH] * proj[:, c2 + 2 * CONV_WIDTH:c2 + 3 * CONV_WIDTH]
        cat = jnp.concatenate([prev, u], axis=0)
        u1 = pltpu.roll(cat, 1, axis=0)[8:]
        u2 = pltpu.roll(cat, 2, axis=0)[8:]
        prev = u[r1 - r0 - 8:]
        if r1 == t:
            carry_ref[...] = prev
        y = cb_ref[...] + cw_ref[0:1, :] * u2 + cw_ref[1:2, :] * u1 + cw_ref[2:3, :] * u
        y = gate_b * y
        for cc in range(CONV_WIDTH // LANES):
            sl = slice(cc * LANES, (cc + 1) * LANES)
            yv = y[:, sl]
            sq = yv * yv
            ms_lo = jnp.sum(jnp.where(lo, sq, 0.0), axis=-1, keepdims=True)
            ms_hi = jnp.sum(jnp.where(lo, 0.0, sq), axis=-1, keepdims=True)
            ms = jnp.where(lo, ms_lo, ms_hi) * (1.0 / (CONV_WIDTH // CONV_GROUPS))
            yc_ref[0, r0:r1, sl] = (yv * lax.rsqrt(ms + RMS_EPS) * og_ref[:, sl]).astype(BF16)


def _mixer_proj(x, mod, tab, win, qg, wq, kvg, wk, wvt, cw, cb, og):
    b, s, d = x.shape
    t = PROJ_TILE
    nt = s // t
    const = lambda shape: pl.BlockSpec(shape, lambda bi, si: (0,) * len(shape))
    return pl.pallas_call(
        _proj_kernel,
        grid=(b, nt),
        in_specs=[pl.BlockSpec((1, t, d), lambda bi, si: (bi, si, 0)),
                  pl.BlockSpec((1, 6, d), lambda bi, si: (bi, 0, 0)),
                  pl.BlockSpec((1, t, LANES), lambda bi, si: (bi, si, 0)),
                  const(win.shape), const(qg.shape), const(wq.shape), const(kvg.shape),
                  const(wk.shape), const(wvt.shape), const(cw.shape), const(cb.shape), const(og.shape)],
        out_specs=[pl.BlockSpec((1, t, MLA_HEADS * HEAD_LANES), lambda bi, si: (bi, si, 0)),
                   pl.BlockSpec((1, t, MLA_HEADS * HEAD_LANES), lambda bi, si: (bi, si, 0)),
                   pl.BlockSpec((1, ATTN_WIDTH, t), lambda bi, si: (bi, 0, si)),
                   pl.BlockSpec((1, t, CONV_WIDTH), lambda bi, si: (bi, si, 0))],
        out_shape=[jax.ShapeDtypeStruct((b, s, MLA_HEADS * HEAD_LANES), BF16),
                   jax.ShapeDtypeStruct((b, s, MLA_HEADS * HEAD_LANES), BF16),
                   jax.ShapeDtypeStruct((b, ATTN_WIDTH, s), BF16),
                   jax.ShapeDtypeStruct((b, s, CONV_WIDTH), BF16)],
        scratch_shapes=[pltpu.VMEM((8, CONV_WIDTH), F32)],
        compiler_params=pltpu.CompilerParams(dimension_semantics=("arbitrary", "arbitrary"),
                                             vmem_limit_bytes=56 * 1024 * 1024, flags=SCHED_FLAGS),
        name="mixer_proj",
    )(x, mod, tab, win, qg, wq, kvg, wk, wvt, cw, cb, og)


def _attn_kernel(q_ref, k_ref, vt_ref, g_ref, o_ref, s_ref, p0_ref, p1_ref, ot_ref):
    seq = q_ref.shape[1]
    qry_chunk = lax.broadcasted_iota(jnp.int32, (1, Q_TILE), 1) // CHUNK
    ones = jnp.ones((16, K_TILE), BF16)
    zero = jnp.minimum(pl.program_id(0), 0)
    dn = (((1,), (1,)), ((), ()))

    def scores(qi, hh):
        q0 = qi * Q_TILE
        hl = slice(hh * HEAD_LANES, (hh + 1) * HEAD_LANES)
        q = q_ref[0, q0:q0 + Q_TILE, hl]
        if qi > 0:
            s_ref[hh, 0:q0, :] = lax.dot_general(k_ref[0, 0:q0, hl], q, dn, preferred_element_type=F32)
        for half in range(2):
            nk = (half + 1) * HALF
            sd = lax.dot_general(k_ref[0, q0:q0 + nk, hl], q[half * HALF:(half + 1) * HALF], dn,
                                 preferred_element_type=F32)
            cols = slice(half * HALF, (half + 1) * HALF)
            for kc in range(nk // CHUNK):
                rows = slice(kc * CHUNK, (kc + 1) * CHUNK)
                vis = qry_chunk[:, cols] >= kc
                s_ref[hh, q0 + kc * CHUNK:q0 + (kc + 1) * CHUNK, cols] = jnp.where(vis, sd[rows], NEG)

    def softmax_pv(qi, hh):
        kmax = (qi + 1) * Q_TILE
        n_full = kmax // K_TILE - 1
        last = slice(n_full * K_TILE, kmax)
        right = slice(HALF, Q_TILE)
        m = None
        for j in range(n_full):
            mj = jnp.max(s_ref[hh, j * K_TILE:(j + 1) * K_TILE, :], axis=0, keepdims=True)
            m = mj if m is None else jnp.maximum(m, mj)
        m_last = jnp.max(s_ref[hh, last, right], axis=0, keepdims=True)
        m = jnp.concatenate([m[:, :HALF], jnp.maximum(m[:, right], m_last)], axis=1)
        p_ref = (p0_ref, p1_ref)[hh]
        o = None
        for j in range(n_full):
            rows = slice(j * K_TILE, (j + 1) * K_TILE)
            p_ref[zero, rows, :] = jnp.exp2(s_ref[hh, rows, :] - m).astype(BF16)
            lhs = jnp.concatenate([vt_ref[0, hh * V_HEAD_DIM:(hh + 1) * V_HEAD_DIM, rows], ones], axis=0)
            oj = jnp.dot(lhs, p_ref[zero, rows, :], preferred_element_type=F32)
            o = oj if o is None else o + oj
        p_ref[zero, last, right] = jnp.exp2(s_ref[hh, last, right] - m[:, right]).astype(BF16)
        lhs = jnp.concatenate([vt_ref[0, hh * V_HEAD_DIM:(hh + 1) * V_HEAD_DIM, last], ones], axis=0)
        o_last = jnp.dot(lhs, p_ref[zero, last, right], preferred_element_type=F32)
        o = jnp.concatenate([o[:, :HALF], o[:, right] + o_last], axis=1)
        o = o[0:V_HEAD_DIM] / o[V_HEAD_DIM:V_HEAD_DIM + 1]
        ms = jnp.mean(o * o, axis=0, keepdims=True)
        ot_ref[hh * V_HEAD_DIM:(hh + 1) * V_HEAD_DIM, :] = o * lax.rsqrt(ms + RMS_EPS)

    chains = [(qi, hh) for qi in range(seq // Q_TILE) for hh in range(2)]
    scores(*chains[0])
    for n, (qi, hh) in enumerate(chains):
        if n + 1 < len(chains):
            scores(*chains[n + 1])
        softmax_pv(qi, hh)
        if hh == 1:
            o_ref[0, qi * Q_TILE:(qi + 1) * Q_TILE, :] = (ot_ref[...].T * g_ref[...]).astype(BF16)


def _attention(q, k, vt, g):
    b, s, _ = q.shape
    pairs = MLA_HEADS // 2
    return pl.pallas_call(
        _attn_kernel,
        grid=(b, pairs),
        in_specs=[pl.BlockSpec((1, s, 2 * HEAD_LANES), lambda bi, pi: (bi, 0, pi)),
                  pl.BlockSpec((1, s, 2 * HEAD_LANES), lambda bi, pi: (bi, 0, pi)),
                  pl.BlockSpec((1, 2 * V_HEAD_DIM, s), lambda bi, pi: (bi, pi, 0)),
                  pl.BlockSpec((1, 2 * V_HEAD_DIM), lambda bi, pi: (0, pi))],
        out_specs=pl.BlockSpec((1, s, 2 * V_HEAD_DIM), lambda bi, pi: (bi, 0, pi)),
        out_shape=jax.ShapeDtypeStruct((b, s, ATTN_WIDTH), BF16),
        scratch_shapes=[pltpu.VMEM((2, s, Q_TILE), F32),
                        pltpu.VMEM((1, s, Q_TILE), BF16),
                        pltpu.VMEM((1, s, Q_TILE), BF16),
                        pltpu.VMEM((2 * V_HEAD_DIM, Q_TILE), F32)],
        compiler_params=pltpu.CompilerParams(dimension_semantics=("arbitrary",) * 2,
                                             vmem_limit_bytes=48 * 1024 * 1024, flags=SCHED_FLAGS),
        name="attention",
    )(q, k, vt, g)


def _ffn_kernel(x_ref, ya_ref, yc_ref, mod_ref, woa_ref, woc_ref, ln1g_ref, ln1b_ref, wup_ref,
                fcw_ref, fcb_ref, wdn_ref, ln2g_ref, ln2b_ref, o_ref, carry_ref, act_ref):
    t = TOK_TILE
    si = pl.program_id(1)
    gate_m = mod_ref[0, 2:3, :]
    shift_f = mod_ref[0, 3:4, :]
    scale_f = mod_ref[0, 4:5, :]
    gate_f = mod_ref[0, 5:6, :]

    @pl.when(si == 0)
    def _():
        carry_ref[...] = jnp.zeros(carry_ref.shape, F32)

    blocks = [(r, r + t // FFN_ROW_BLOCKS) for r in range(0, t, t // FFN_ROW_BLOCKS)]
    x1s, hs = [], []
    for r0, r1 in blocks:
        mix = (jnp.dot(ya_ref[0, r0:r1, :], woa_ref[...], preferred_element_type=F32)
               + jnp.dot(yc_ref[0, r0:r1, :], woc_ref[...], preferred_element_type=F32))
        x1 = _layer_norm(DEEPNORM_ALPHA * x_ref[0, r0:r1, :] + gate_m * mix) * ln1g_ref[...] + ln1b_ref[...]
        x1s.append(x1)
        hs.append((_layer_norm(x1) * (1.0 + scale_f) + shift_f).astype(BF16))

    def conv_cols(cols, h, prev, last):
        u = jnp.dot(h, wup_ref[:, cols], preferred_element_type=F32)
        cat = jnp.concatenate([prev, u], axis=0)
        u1 = pltpu.roll(cat, 1, axis=0)[8:]
        u2 = pltpu.roll(cat, 2, axis=0)[8:]
        tail = u[u.shape[0] - 8:]
        if last:
            carry_ref[:, cols] = tail
        y = fcb_ref[:, cols] + fcw_ref[0:1, cols] * u2 + fcw_ref[1:2, cols] * u1 + fcw_ref[2:3, cols] * u
        return y, tail

    for c in range(N_FF_CHUNKS):
        gcols = slice(c * FF_CHUNK, (c + 1) * FF_CHUNK)
        vcols = slice(D_FF + c * FF_CHUNK, D_FF + (c + 1) * FF_CHUNK)
        gprev, vprev = carry_ref[:, gcols], carry_ref[:, vcols]
        for bi, ((r0, r1), h) in enumerate(zip(blocks, hs)):
            last = bi == len(blocks) - 1
            g, gprev = conv_cols(gcols, h, gprev, last)
            v, vprev = conv_cols(vcols, h, vprev, last)
            act_ref[r0:r1, c * FF_CHUNK:(c + 1) * FF_CHUNK] = (g * _sigmoid(g) * v).astype(BF16)

    for (r0, r1), x1 in zip(blocks, x1s):
        ff = jnp.dot(act_ref[r0:r1, :], wdn_ref[...], preferred_element_type=F32)
        o_ref[0, r0:r1, :] = (_layer_norm(DEEPNORM_ALPHA * x1 + gate_f * ff) * ln2g_ref[...]
                              + ln2b_ref[...])


def _out_ffn(x, ya, yc, mod, woa, woc, ln1g, ln1b, wup, fcw, fcb, wdn, ln2g, ln2b):
    b, s, d = x.shape
    t = TOK_TILE
    const = lambda shape: pl.BlockSpec(shape, lambda bi, si: (0,) * len(shape),
                                       pipeline_mode=pl.Buffered(1))
    tok = lambda width: pl.BlockSpec((1, t, width), lambda bi, si: (bi, si, 0))
    return pl.pallas_call(
        _ffn_kernel,
        grid=(b, s // t),
        in_specs=[tok(d), tok(ATTN_WIDTH), tok(CONV_WIDTH),
                  pl.BlockSpec((1, 6, d), lambda bi, si: (bi, 0, 0)),
                  const(woa.shape), const(woc.shape), const(ln1g.shape), const(ln1b.shape),
                  const(wup.shape), const(fcw.shape), const(fcb.shape), const(wdn.shape),
                  const(ln2g.shape), const(ln2b.shape)],
        out_specs=tok(d),
        out_shape=jax.ShapeDtypeStruct((b, s, d), F32),
        scratch_shapes=[pltpu.VMEM((8, 2 * D_FF), F32),
                        pltpu.VMEM((t, D_FF), BF16)],
        compiler_params=pltpu.CompilerParams(dimension_semantics=("arbitrary", "arbitrary"),
                                             vmem_limit_bytes=56 * 1024 * 1024, flags=SCHED_FLAGS),
        name="out_ffn",
    )(x, ya, yc, mod, woa, woc, ln1g, ln1b, wup, fcw, fcb, wdn, ln2g, ln2b)


def _swap_halves(w):
    half = w.shape[-1] // 2
    return jnp.concatenate([w[..., half:], w[..., :half]], axis=-1)


def _prep_in_proj(w_in):
    c0 = Q_LORA_RANK + KV_LORA_RANK
    kr = w_in[:, c0:c0 + QK_ROPE_DIM]
    krs = _swap_halves(kr)
    return jnp.concatenate([w_in[:, :c0], kr, krs, kr, krs, w_in[:, c0 + QK_ROPE_DIM:]], axis=1).astype(BF16)


def _prep_q_up(w_q_up):
    w = w_q_up.reshape(Q_LORA_RANK, MLA_HEADS, QK_NOPE_DIM + QK_ROPE_DIM)
    rope = w[..., QK_NOPE_DIM:]
    w = jnp.concatenate([w[..., :QK_NOPE_DIM], rope, _swap_halves(rope)], axis=-1)
    return w.reshape(Q_LORA_RANK, MLA_HEADS * HEAD_LANES).astype(BF16)


def _prep_kv_up(w_kv_up):
    w = w_kv_up.reshape(KV_LORA_RANK, MLA_HEADS, QK_NOPE_DIM + V_HEAD_DIM)
    wk = w[..., :QK_NOPE_DIM].reshape(KV_LORA_RANK, MLA_HEADS * QK_NOPE_DIM)
    wvt = w[..., QK_NOPE_DIM:].reshape(KV_LORA_RANK, ATTN_WIDTH).T
    return wk.astype(BF16), wvt.astype(BF16)


def kernel(x, c, positions, w_ada, b_ada, w_in, q_norm_g, w_q_up, kv_norm_g, w_kv_up, conv_w, conv_b,
           out_norm_g, w_out, ln1_g, ln1_b, w_up, ffn_conv_w, ffn_conv_b, w_down, ln2_g, ln2_b):
    b, s, d = x.shape
    depth = w_ada.shape[0]
    tab = _rope_table(positions)
    for l in range(depth):
        mod = _adaln_mod(c, w_ada[l], b_ada[l]).reshape(b, 6, d)
        wk, wvt = _prep_kv_up(w_kv_up[l])
        q, k, vt, yc = _mixer_proj(
            x, mod, tab, _prep_in_proj(w_in[l]), q_norm_g[l].reshape(1, -1), _prep_q_up(w_q_up[l]),
            kv_norm_g[l].reshape(1, -1), wk, wvt, conv_w[l], conv_b[l].reshape(1, -1),
            out_norm_g[l, ATTN_WIDTH:].reshape(1, -1))
        ya = _attention(q, k, vt, out_norm_g[l, :ATTN_WIDTH].reshape(1, -1))
        x = _out_ffn(
            x, ya, yc, mod, w_out[l, :ATTN_WIDTH].astype(BF16), w_out[l, ATTN_WIDTH:].astype(BF16),
            ln1_g[l].reshape(1, -1), ln1_b[l].reshape(1, -1), w_up[l].astype(BF16),
            ffn_conv_w[l], ffn_conv_b[l].reshape(1, -1),
            w_down[l].astype(BF16), ln2_g[l].reshape(1, -1), ln2_b[l].reshape(1, -1))
    return x
```

```python
import jax
import jax.numpy as jnp
from jax import lax
from jax.experimental import pallas as pl
from jax.experimental.pallas import tpu as pltpu

D_MODEL = 1024
SEQ = 2048
CHUNK = 64
MLA_HEADS = 8
QK_NOPE_DIM = 64
QK_ROPE_DIM = 32
V_HEAD_DIM = 64
Q_LORA_RANK = 256
KV_LORA_RANK = 128
ATTN_WIDTH = MLA_HEADS * V_HEAD_DIM
CONV_WIDTH = D_MODEL - ATTN_WIDTH
CONV_GROUPS = 8
D_FF = 2816
ROPE_THETA = 10000.0
RMS_EPS = 1e-6
LN_EPS = 1e-5
DEEPNORM_ALPHA = 2.0 ** 0.25

LANES = 128
HEAD_LANES = 128
IN_EXT = Q_LORA_RANK + KV_LORA_RANK + LANES + 3 * CONV_WIDTH
TOK_TILE = 512
PROJ_TILE = 1024
Q_TILE = 512
K_TILE = 256
HALF = Q_TILE // 2
ATTN_HEADS = 4
assert K_TILE == HALF
FF_CHUNK = 256
N_FF_CHUNKS = D_FF // FF_CHUNK
FFN_ROW_BLOCKS = 2
PROJ_ROW_BLOCKS = 4
ADALN_K_BLOCK = 256
NEG = -1e30
Q_SCALE = (QK_NOPE_DIM + QK_ROPE_DIM) ** -0.5 * 1.4426950408889634

F32 = jnp.float32
BF16 = jnp.bfloat16
SCHED_FLAGS = None


def _layer_norm(x):
    mu = jnp.mean(x, axis=-1, keepdims=True)
    xc = x - mu
    var = jnp.mean(xc * xc, axis=-1, keepdims=True)
    return xc * lax.rsqrt(var + LN_EPS)


def _rms(x):
    return x * lax.rsqrt(jnp.mean(x * x, axis=-1, keepdims=True) + RMS_EPS)


def _sigmoid(x):
    return 1.0 / (1.0 + jnp.exp(-x))


def _split3(x):
    hi = x.astype(BF16)
    r1 = x - hi.astype(F32)
    mid = r1.astype(BF16)
    lo = (r1 - mid.astype(F32)).astype(BF16)
    return hi, mid, lo


def _rope_kernel(pos_ref, freq_ref, e_ref, tab_ref):
    ang = freq_ref[...] * pos_ref[0].astype(F32)
    pieces = jnp.concatenate(_split3(jnp.cos(ang)) + _split3(jnp.sin(ang)), axis=0)
    tab_ref[0] = lax.dot_general(pieces, e_ref[...], (((0,), (0,)), ((), ())),
                                 preferred_element_type=F32)


def _rope_expansion():
    half = QK_ROPE_DIM // 2
    src = jnp.arange(6 * half)[:, None]
    dst = jnp.arange(LANES)[None, :]
    same_freq = src % half == dst % half
    is_sin_row = src >= 3 * half
    group = dst // half % 4
    e_cos = jnp.where(same_freq & ~is_sin_row & (group < 2), 1.0, 0.0)
    e_sin = jnp.where(same_freq & is_sin_row & (group == 2), -1.0,
                      jnp.where(same_freq & is_sin_row & (group == 3), 1.0, 0.0))
    return (e_cos + e_sin).astype(BF16)


def _rope_table(positions):
    b, s = positions.shape
    half = QK_ROPE_DIM // 2
    inv_freq = ROPE_THETA ** (-jnp.arange(0, QK_ROPE_DIM, 2, dtype=F32) / QK_ROPE_DIM)
    e = _rope_expansion()
    return pl.pallas_call(
        _rope_kernel,
        grid=(b,),
        in_specs=[pl.BlockSpec((1, 1, s), lambda i: (i, 0, 0)),
                  pl.BlockSpec((half, 1), lambda i: (0, 0)),
                  pl.BlockSpec(e.shape, lambda i: (0, 0))],
        out_specs=pl.BlockSpec((1, s, LANES), lambda i: (i, 0, 0)),
        out_shape=jax.ShapeDtypeStruct((b, s, LANES), F32),
        compiler_params=pltpu.CompilerParams(dimension_semantics=("arbitrary",),
                                             vmem_limit_bytes=32 * 1024 * 1024),
        name="rope_table",
    )(positions.reshape(b, 1, s), inv_freq.reshape(half, 1), e)


def _mod_kernel(c_ref, w_ref, b_ref, o_ref):
    @pl.when(pl.program_id(0) == 0)
    def _():
        o_ref[...] = jnp.broadcast_to(b_ref[...], o_ref.shape)

    c = c_ref[...]
    act = c * _sigmoid(c)
    o_ref[...] += jnp.dot(act, w_ref[...], preferred_element_type=F32, precision=lax.Precision.HIGHEST)


def _adaln_mod(c, w_ada, b_ada):
    b, d = c.shape
    n = w_ada.shape[1]
    kb = ADALN_K_BLOCK
    return pl.pallas_call(
        _mod_kernel,
        grid=(d // kb,),
        in_specs=[pl.BlockSpec((b, kb), lambda j: (0, j)),
                  pl.BlockSpec((kb, n), lambda j: (j, 0)),
                  pl.BlockSpec((1, n), lambda j: (0, 0))],
        out_specs=pl.BlockSpec((b, n), lambda j: (0, 0)),
        out_shape=jax.ShapeDtypeStruct((b, n), F32),
        compiler_params=pltpu.CompilerParams(dimension_semantics=("arbitrary",),
                                             vmem_limit_bytes=32 * 1024 * 1024),
        name="adaln_mod",
    )(c, w_ada, b_ada.reshape(1, n))


def _proj_kernel(x_ref, mod_ref, tab_ref, win_ref, qg_ref, wq_ref, kvg_ref, wk_ref, wvt_ref,
                 cw_ref, cb_ref, og_ref, q_ref, k_ref, vt_ref, yc_ref, carry_ref):
    t = PROJ_TILE
    si = pl.program_id(1)
    shift = mod_ref[0, 0:1, :]
    scale = mod_ref[0, 1:2, :]
    lane = lax.broadcasted_iota(jnp.int32, (1, LANES), 1)
    is_nope = lane < QK_NOPE_DIM
    lo = lane < (CONV_WIDTH // CONV_GROUPS)
    c0 = Q_LORA_RANK
    c1 = c0 + KV_LORA_RANK
    c2 = c1 + LANES

    @pl.when(si == 0)
    def _():
        carry_ref[...] = jnp.zeros((8, CONV_WIDTH), F32)

    prev = carry_ref[...]
    for r0 in range(0, t, t // PROJ_ROW_BLOCKS):
        r1 = r0 + t // PROJ_ROW_BLOCKS
        h = (_layer_norm(x_ref[0, r0:r1, :]) * (1.0 + scale) + shift).astype(BF16)
        proj = jnp.dot(h, win_ref[...], preferred_element_type=F32)
        tab = tab_ref[0, r0:r1, :]

        cq = (_rms(proj[:, 0:Q_LORA_RANK]) * qg_ref[...]).astype(BF16)
        q = jnp.dot(cq, wq_ref[...], preferred_element_type=F32)
        q_tab = jnp.where(is_nope, Q_SCALE, Q_SCALE * tab)
        for hh in range(MLA_HEADS):
            sl = slice(hh * HEAD_LANES, (hh + 1) * HEAD_LANES)
            q_ref[0, r0:r1, sl] = (q[:, sl] * q_tab).astype(BF16)

        ckv = (_rms(proj[:, c0:c0 + KV_LORA_RANK]) * kvg_ref[...]).astype(BF16)
        knope = jnp.dot(ckv, wk_ref[...], preferred_element_type=F32)
        r = proj[:, c1:c1 + LANES] * tab
        kf = r + pltpu.roll(r, QK_ROPE_DIM, axis=1)
        for hp in range(MLA_HEADS // 2):
            pair = knope[:, hp * LANES:(hp + 1) * LANES]
            for hh, nope in ((2 * hp, pair), (2 * hp + 1, pltpu.roll(pair, QK_NOPE_DIM, axis=1))):
                sl = slice(hh * HEAD_LANES, (hh + 1) * HEAD_LANES)
                k_ref[0, r0:r1, sl] = jnp.where(is_nope, nope, kf).astype(BF16)
        vt_ref[0, :, r0:r1] = lax.dot_general(wvt_ref[...], ckv, (((1,), (1,)), ((), ())),
                                              preferred_element_type=F32).astype(BF16)

        gate_b = proj[:, c2:c2 + CONV_WIDTH]
        u = proj[:, c2 + CONV_WIDTH:c2 + 2 * CONV_WIDTH] * proj[:, c2 + 2 * CONV_WIDTH:c2 + 3 * CONV_WIDTH]
        cat = jnp.concatenate([prev, u], axis=0)
        u1 = pltpu.roll(cat, 1, axis=0)[8:]
        u2 = pltpu.roll(cat, 2, axis=0)[8:]
        prev = u[r1 - r0 - 8:]
        if r1 == t:
            carry_ref[...] = prev
        y = cb_ref[...] + cw_ref[0:1, :] * u2 + cw_ref[1:2, :] * u1 + cw_ref[2:3, :] * u
        y = gate_b * y
        for cc in range(CONV_WIDTH // LANES):
            sl = slice(cc * LANES, (cc + 1) * LANES)
            yv = y[:, sl]
            sq = yv * yv
            ms_lo = jnp.sum(jnp.where(lo, sq, 0.0), axis=-1, keepdims=True)
            ms_hi = jnp.sum(jnp.where(lo, 0.0, sq), axis=-1, keepdims=True)
            ms = jnp.where(lo, ms_lo, ms_hi) * (1.0 / (CONV_WIDTH // CONV_GROUPS))
            yc_ref[0, r0:r1, sl] = (yv * lax.rsqrt(ms + RMS_EPS) * og_ref[:, sl]).astype(BF16)


def _mixer_proj(x, mod, tab, win, qg, wq, kvg, wk, wvt, cw, cb, og):
    b, s, d = x.shape
    t = PROJ_TILE
    nt = s // t
    const = lambda shape: pl.BlockSpec(shape, lambda bi, si: (0,) * len(shape))
    return pl.pallas_call(
        _proj_kernel,
        grid=(b, nt),
        in_specs=[pl.BlockSpec((1, t, d), lambda bi, si: (bi, si, 0)),
                  pl.BlockSpec((1, 6, d), lambda bi, si: (bi, 0, 0)),
                  pl.BlockSpec((1, t, LANES), lambda bi, si: (bi, si, 0)),
                  const(win.shape), const(qg.shape), const(wq.shape), const(kvg.shape),
                  const(wk.shape), const(wvt.shape), const(cw.shape), const(cb.shape), const(og.shape)],
        out_specs=[pl.BlockSpec((1, t, MLA_HEADS * HEAD_LANES), lambda bi, si: (bi, si, 0)),
                   pl.BlockSpec((1, t, MLA_HEADS * HEAD_LANES), lambda bi, si: (bi, si, 0)),
                   pl.BlockSpec((1, ATTN_WIDTH, t), lambda bi, si: (bi, 0, si)),
                   pl.BlockSpec((1, t, CONV_WIDTH), lambda bi, si: (bi, si, 0))],
        out_shape=[jax.ShapeDtypeStruct((b, s, MLA_HEADS * HEAD_LANES), BF16),
                   jax.ShapeDtypeStruct((b, s, MLA_HEADS * HEAD_LANES), BF16),
                   jax.ShapeDtypeStruct((b, ATTN_WIDTH, s), BF16),
                   jax.ShapeDtypeStruct((b, s, CONV_WIDTH), BF16)],
        scratch_shapes=[pltpu.VMEM((8, CONV_WIDTH), F32)],
        compiler_params=pltpu.CompilerParams(dimension_semantics=("arbitrary", "arbitrary"),
                                             vmem_limit_bytes=56 * 1024 * 1024, flags=SCHED_FLAGS),
        name="mixer_proj",
    )(x, mod, tab, win, qg, wq, kvg, wk, wvt, cw, cb, og)


def _attn_kernel(q_ref, k_ref, vt_ref, g_ref, o_ref, s_ref, ot_ref):
    seq = q_ref.shape[1]
    qry_chunk = lax.broadcasted_iota(jnp.int32, (1, Q_TILE), 1) // CHUNK
    ones = jnp.ones((16, K_TILE), BF16)
    dn = (((1,), (1,)), ((), ()))

    def scores(qi, hh):
        q0 = qi * Q_TILE
        hl = slice(hh * HEAD_LANES, (hh + 1) * HEAD_LANES)
        q = q_ref[0, q0:q0 + Q_TILE, hl]
        if qi > 0:
            s_ref[hh % 2, 0:q0, :] = lax.dot_general(k_ref[0, 0:q0, hl], q, dn, preferred_element_type=F32)
        for half in range(2):
            nk = (half + 1) * HALF
            sd = lax.dot_general(k_ref[0, q0:q0 + nk, hl], q[half * HALF:(half + 1) * HALF], dn,
                                 preferred_element_type=F32)
            cols = slice(half * HALF, (half + 1) * HALF)
            for kc in range(nk // CHUNK):
                rows = slice(kc * CHUNK, (kc + 1) * CHUNK)
                vis = qry_chunk[:, cols] >= kc
                s_ref[hh % 2, q0 + kc * CHUNK:q0 + (kc + 1) * CHUNK, cols] = jnp.where(vis, sd[rows], NEG)

    def softmax_pv(qi, hh):
        kmax = (qi + 1) * Q_TILE
        n_full = kmax // K_TILE - 1
        last = slice(n_full * K_TILE, kmax)
        right = slice(HALF, Q_TILE)
        m = None
        for j in range(n_full):
            mj = jnp.max(s_ref[hh % 2, j * K_TILE:(j + 1) * K_TILE, :], axis=0, keepdims=True)
            m = mj if m is None else jnp.maximum(m, mj)
        m_last = jnp.max(s_ref[hh % 2, last, right], axis=0, keepdims=True)
        m = jnp.concatenate([m[:, :HALF], jnp.maximum(m[:, right], m_last)], axis=1)
        o = None
        for j in range(n_full):
            rows = slice(j * K_TILE, (j + 1) * K_TILE)
            p = jnp.exp2(s_ref[hh % 2, rows, :] - m).astype(BF16)
            lhs = jnp.concatenate([vt_ref[0, hh * V_HEAD_DIM:(hh + 1) * V_HEAD_DIM, rows], ones], axis=0)
            oj = jnp.dot(lhs, p, preferred_element_type=F32)
            o = oj if o is None else o + oj
        p = jnp.exp2(s_ref[hh % 2, last, right] - m[:, right]).astype(BF16)
        lhs = jnp.concatenate([vt_ref[0, hh * V_HEAD_DIM:(hh + 1) * V_HEAD_DIM, last], ones], axis=0)
        o = jnp.concatenate([o[:, :HALF], o[:, right] + jnp.dot(lhs, p, preferred_element_type=F32)], axis=1)
        o = o[0:V_HEAD_DIM] / o[V_HEAD_DIM:V_HEAD_DIM + 1]
        ms = jnp.mean(o * o, axis=0, keepdims=True)
        ot_ref[hh * V_HEAD_DIM:(hh + 1) * V_HEAD_DIM, :] = o * lax.rsqrt(ms + RMS_EPS)

    chains = [(qi, hh) for qi in range(seq // Q_TILE) for hh in range(ATTN_HEADS)]
    scores(*chains[0])
    for n, (qi, hh) in enumerate(chains):
        if n + 1 < len(chains):
            scores(*chains[n + 1])
        softmax_pv(qi, hh)
        if hh == ATTN_HEADS - 1:
            o_ref[0, qi * Q_TILE:(qi + 1) * Q_TILE, :] = (ot_ref[...].T * g_ref[...]).astype(BF16)


def _attention(q, k, vt, g):
    b, s, _ = q.shape
    n_h = ATTN_HEADS
    return pl.pallas_call(
        _attn_kernel,
        grid=(b, MLA_HEADS // n_h),
        in_specs=[pl.BlockSpec((1, s, n_h * HEAD_LANES), lambda bi, pi: (bi, 0, pi)),
                  pl.BlockSpec((1, s, n_h * HEAD_LANES), lambda bi, pi: (bi, 0, pi)),
                  pl.BlockSpec((1, n_h * V_HEAD_DIM, s), lambda bi, pi: (bi, pi, 0)),
                  pl.BlockSpec((1, n_h * V_HEAD_DIM), lambda bi, pi: (0, pi))],
        out_specs=pl.BlockSpec((1, s, n_h * V_HEAD_DIM), lambda bi, pi: (bi, 0, pi)),
        out_shape=jax.ShapeDtypeStruct((b, s, ATTN_WIDTH), BF16),
        scratch_shapes=[pltpu.VMEM((2, s, Q_TILE), F32),
                        pltpu.VMEM((n_h * V_HEAD_DIM, Q_TILE), F32)],
        compiler_params=pltpu.CompilerParams(dimension_semantics=("arbitrary",) * 2,
                                             vmem_limit_bytes=48 * 1024 * 1024, flags=SCHED_FLAGS),
        name="attention",
    )(q, k, vt, g)


def _ffn_kernel(x_ref, ya_ref, yc_ref, mod_ref, woa_ref, woc_ref, ln1g_ref, ln1b_ref, wup_ref,
                fcw_ref, fcb_ref, wdn_ref, ln2g_ref, ln2b_ref, o_ref, carry_ref, act_ref):
    t = TOK_TILE
    si = pl.program_id(1)
    gate_m = mod_ref[0, 2:3, :]
    shift_f = mod_ref[0, 3:4, :]
    scale_f = mod_ref[0, 4:5, :]
    gate_f = mod_ref[0, 5:6, :]

    @pl.when(si == 0)
    def _():
        carry_ref[...] = jnp.zeros(carry_ref.shape, F32)

    blocks = [(r, r + t // FFN_ROW_BLOCKS) for r in range(0, t, t // FFN_ROW_BLOCKS)]
    x1s, hs = [], []
    for r0, r1 in blocks:
        mix = (jnp.dot(ya_ref[0, r0:r1, :], woa_ref[...], preferred_element_type=F32)
               + jnp.dot(yc_ref[0, r0:r1, :], woc_ref[...], preferred_element_type=F32))
        x1 = _layer_norm(DEEPNORM_ALPHA * x_ref[0, r0:r1, :] + gate_m * mix) * ln1g_ref[...] + ln1b_ref[...]
        x1s.append(x1)
        hs.append((_layer_norm(x1) * (1.0 + scale_f) + shift_f).astype(BF16))

    def conv_cols(cols, h, prev, last):
        u = jnp.dot(h, wup_ref[:, cols], preferred_element_type=F32)
        cat = jnp.concatenate([prev, u], axis=0)
        u1 = pltpu.roll(cat, 1, axis=0)[8:]
        u2 = pltpu.roll(cat, 2, axis=0)[8:]
        tail = u[u.shape[0] - 8:]
        if last:
            carry_ref[:, cols] = tail
        y = fcb_ref[:, cols] + fcw_ref[0:1, cols] * u2 + fcw_ref[1:2, cols] * u1 + fcw_ref[2:3, cols] * u
        return y, tail

    for c in range(N_FF_CHUNKS):
        gcols = slice(c * FF_CHUNK, (c + 1) * FF_CHUNK)
        vcols = slice(D_FF + c * FF_CHUNK, D_FF + (c + 1) * FF_CHUNK)
        gprev, vprev = carry_ref[:, gcols], carry_ref[:, vcols]
        for bi, ((r0, r1), h) in enumerate(zip(blocks, hs)):
            last = bi == len(blocks) - 1
            g, gprev = conv_cols(gcols, h, gprev, last)
            v, vprev = conv_cols(vcols, h, vprev, last)
            act_ref[r0:r1, c * FF_CHUNK:(c + 1) * FF_CHUNK] = (g * _sigmoid(g) * v).astype(BF16)

    for (r0, r1), x1 in zip(blocks, x1s):
        ff = jnp.dot(act_ref[r0:r1, :], wdn_ref[...], preferred_element_type=F32)
        o_ref[0, r0:r1, :] = (_layer_norm(DEEPNORM_ALPHA * x1 + gate_f * ff) * ln2g_ref[...]
                              + ln2b_ref[...])


def _out_ffn(x, ya, yc, mod, woa, woc, ln1g, ln1b, wup, fcw, fcb, wdn, ln2g, ln2b):
    b, s, d = x.shape
    t = TOK_TILE
    const = lambda shape: pl.BlockSpec(shape, lambda bi, si: (0,) * len(shape),
                                       pipeline_mode=pl.Buffered(1))
    tok = lambda width: pl.BlockSpec((1, t, width), lambda bi, si: (bi, si, 0))
    return pl.pallas_call(
        _ffn_kernel,
        grid=(b, s // t),
        in_specs=[tok(d), tok(ATTN_WIDTH), tok(CONV_WIDTH),
                  pl.BlockSpec((1, 6, d), lambda bi, si: (bi, 0, 0)),
                  const(woa.shape), const(woc.shape), const(ln1g.shape), const(ln1b.shape),
                  const(wup.shape), const(fcw.shape), const(fcb.shape), const(wdn.shape),
                  const(ln2g.shape), const(ln2b.shape)],
        out_specs=tok(d),
        out_shape=jax.ShapeDtypeStruct((b, s, d), F32),
        scratch_shapes=[pltpu.VMEM((8, 2 * D_FF), F32),
                        pltpu.VMEM((t, D_FF), BF16)],
        compiler_params=pltpu.CompilerParams(dimension_semantics=("arbitrary", "arbitrary"),
                                             vmem_limit_bytes=56 * 1024 * 1024, flags=SCHED_FLAGS),
        name="out_ffn",
    )(x, ya, yc, mod, woa, woc, ln1g, ln1b, wup, fcw, fcb, wdn, ln2g, ln2b)


def _swap_halves(w):
    half = w.shape[-1] // 2
    return jnp.concatenate([w[..., half:], w[..., :half]], axis=-1)


def _prep_in_proj(w_in):
    c0 = Q_LORA_RANK + KV_LORA_RANK
    kr = w_in[:, c0:c0 + QK_ROPE_DIM]
    krs = _swap_halves(kr)
    return jnp.concatenate([w_in[:, :c0], kr, krs, kr, krs, w_in[:, c0 + QK_ROPE_DIM:]], axis=1).astype(BF16)


def _prep_q_up(w_q_up):
    w = w_q_up.reshape(Q_LORA_RANK, MLA_HEADS, QK_NOPE_DIM + QK_ROPE_DIM)
    rope = w[..., QK_NOPE_DIM:]
    w = jnp.concatenate([w[..., :QK_NOPE_DIM], rope, _swap_halves(rope)], axis=-1)
    return w.reshape(Q_LORA_RANK, MLA_HEADS * HEAD_LANES).astype(BF16)


def _prep_kv_up(w_kv_up):
    w = w_kv_up.reshape(KV_LORA_RANK, MLA_HEADS, QK_NOPE_DIM + V_HEAD_DIM)
    wk = w[..., :QK_NOPE_DIM].reshape(KV_LORA_RANK, MLA_HEADS * QK_NOPE_DIM)
    wvt = w[..., QK_NOPE_DIM:].reshape(KV_LORA_RANK, ATTN_WIDTH).T
    return wk.astype(BF16), wvt.astype(BF16)


def kernel(x, c, positions, w_ada, b_ada, w_in, q_norm_g, w_q_up, kv_norm_g, w_kv_up, conv_w, conv_b,
           out_norm_g, w_out, ln1_g, ln1_b, w_up, ffn_conv_w, ffn_conv_b, w_down, ln2_g, ln2_b):
    b, s, d = x.shape
    depth = w_ada.shape[0]
    tab = _rope_table(positions)
    for l in range(depth):
        mod = _adaln_mod(c, w_ada[l], b_ada[l]).reshape(b, 6, d)
        wk, wvt = _prep_kv_up(w_kv_up[l])
        q, k, vt, yc = _mixer_proj(
            x, mod, tab, _prep_in_proj(w_in[l]), q_norm_g[l].reshape(1, -1), _prep_q_up(w_q_up[l]),
            kv_norm_g[l].reshape(1, -1), wk, wvt, conv_w[l], conv_b[l].reshape(1, -1),
            out_norm_g[l, ATTN_WIDTH:].reshape(1, -1))
        ya = _attention(q, k, vt, out_norm_g[l, :ATTN_WIDTH].reshape(1, -1))
        x = _out_ffn(
            x, ya, yc, mod, w_out[l, :ATTN_WIDTH].astype(BF16), w_out[l, ATTN_WIDTH:].astype(BF16),
            ln1_g[l].reshape(1, -1), ln1_b[l].reshape(1, -1), w_up[l].astype(BF16),
            ffn_conv_w[l], ffn_conv_b[l].reshape(1, -1),
            w_down[l].astype(BF16), ln2_g[l].reshape(1, -1), ln2_b[l].reshape(1, -1))
    return x
```

```python
import jax
import jax.numpy as jnp
from jax import lax
from jax.experimental import pallas as pl
from jax.experimental.pallas import tpu as pltpu

D_MODEL = 1024
SEQ = 2048
CHUNK = 64
MLA_HEADS = 8
QK_NOPE_DIM = 64
QK_ROPE_DIM = 32
V_HEAD_DIM = 64
Q_LORA_RANK = 256
KV_LORA_RANK = 128
ATTN_WIDTH = MLA_HEADS * V_HEAD_DIM
CONV_WIDTH = D_MODEL - ATTN_WIDTH
CONV_GROUPS = 8
D_FF = 2816
ROPE_THETA = 10000.0
RMS_EPS = 1e-6
LN_EPS = 1e-5
DEEPNORM_ALPHA = 2.0 ** 0.25

LANES = 128
HEAD_LANES = 128
IN_EXT = Q_LORA_RANK + KV_LORA_RANK + LANES + 3 * CONV_WIDTH
TOK_TILE = 512
PROJ_TILE = 1024
Q_TILE = 512
K_TILE = 256
HALF = Q_TILE // 2
ATTN_HEADS = 4
assert K_TILE == HALF
FF_CHUNK = 256
N_FF_CHUNKS = D_FF // FF_CHUNK
FFN_ROW_BLOCKS = 2
PROJ_ROW_BLOCKS = 4
ADALN_K_BLOCK = 256
NEG = -1e30
Q_SCALE = (QK_NOPE_DIM + QK_ROPE_DIM) ** -0.5 * 1.4426950408889634

F32 = jnp.float32
BF16 = jnp.bfloat16
SCHED_FLAGS = None


def _layer_norm(x):
    mu = jnp.mean(x, axis=-1, keepdims=True)
    xc = x - mu
    var = jnp.mean(xc * xc, axis=-1, keepdims=True)
    return xc * lax.rsqrt(var + LN_EPS)


def _rms(x):
    return x * lax.rsqrt(jnp.mean(x * x, axis=-1, keepdims=True) + RMS_EPS)


def _sigmoid(x):
    return 1.0 / (1.0 + jnp.exp(-x))


def _split3(x):
    hi = x.astype(BF16)
    r1 = x - hi.astype(F32)
    mid = r1.astype(BF16)
    lo = (r1 - mid.astype(F32)).astype(BF16)
    return hi, mid, lo


def _rope_rows(pos, freq, e):
    ang = freq * pos.astype(F32)
    pieces = jnp.concatenate(_split3(jnp.cos(ang)) + _split3(jnp.sin(ang)), axis=0)
    return lax.dot_general(pieces, e, (((0,), (0,)), ((), ())), preferred_element_type=F32)


def _rope_expansion():
    half = QK_ROPE_DIM // 2
    src = jnp.arange(6 * half)[:, None]
    dst = jnp.arange(LANES)[None, :]
    same_freq = src % half == dst % half
    is_sin_row = src >= 3 * half
    group = dst // half % 4
    e_cos = jnp.where(same_freq & ~is_sin_row & (group < 2), 1.0, 0.0)
    e_sin = jnp.where(same_freq & is_sin_row & (group == 2), -1.0,
                      jnp.where(same_freq & is_sin_row & (group == 3), 1.0, 0.0))
    return (e_cos + e_sin).astype(BF16)


def _mod_kernel(c_ref, w_ref, b_ref, o_ref):
    @pl.when(pl.program_id(0) == 0)
    def _():
        o_ref[...] = jnp.broadcast_to(b_ref[...], o_ref.shape)

    c = c_ref[...]
    act = c * _sigmoid(c)
    o_ref[...] += jnp.dot(act, w_ref[...], preferred_element_type=F32, precision=lax.Precision.HIGHEST)


def _adaln_mod(c, w_ada, b_ada):
    b, d = c.shape
    n = w_ada.shape[1]
    kb = ADALN_K_BLOCK
    return pl.pallas_call(
        _mod_kernel,
        grid=(d // kb,),
        in_specs=[pl.BlockSpec((b, kb), lambda j: (0, j)),
                  pl.BlockSpec((kb, n), lambda j: (j, 0)),
                  pl.BlockSpec((1, n), lambda j: (0, 0))],
        out_specs=pl.BlockSpec((b, n), lambda j: (0, 0)),
        out_shape=jax.ShapeDtypeStruct((b, n), F32),
        compiler_params=pltpu.CompilerParams(dimension_semantics=("arbitrary",),
                                             vmem_limit_bytes=32 * 1024 * 1024),
        name="adaln_mod",
    )(c, w_ada, b_ada.reshape(1, n))


def _proj_kernel(x_ref, mod_ref, pos_ref, freq_ref, e_ref, win_ref, qg_ref, wq_ref, kvg_ref, wk_ref, wvt_ref,
                 cw_ref, cb_ref, og_ref, q_ref, k_ref, vt_ref, yc_ref, carry_ref):
    t = PROJ_TILE
    si = pl.program_id(1)
    shift = mod_ref[0, 0:1, :]
    scale = mod_ref[0, 1:2, :]
    lane = lax.broadcasted_iota(jnp.int32, (1, LANES), 1)
    is_nope = lane < QK_NOPE_DIM
    lo = lane < (CONV_WIDTH // CONV_GROUPS)
    c0 = Q_LORA_RANK
    c1 = c0 + KV_LORA_RANK
    c2 = c1 + LANES

    @pl.when(si == 0)
    def _():
        carry_ref[...] = jnp.zeros((8, CONV_WIDTH), F32)

    prev = carry_ref[...]
    for r0 in range(0, t, t // PROJ_ROW_BLOCKS):
        r1 = r0 + t // PROJ_ROW_BLOCKS
        h = (_layer_norm(x_ref[0, r0:r1, :]) * (1.0 + scale) + shift).astype(BF16)
        proj = jnp.dot(h, win_ref[...], preferred_element_type=F32)
        tab = _rope_rows(pos_ref[0, :, r0:r1], freq_ref[...], e_ref[...])

        cq = (_rms(proj[:, 0:Q_LORA_RANK]) * qg_ref[...]).astype(BF16)
        q = jnp.dot(cq, wq_ref[...], preferred_element_type=F32)
        q_tab = jnp.where(is_nope, Q_SCALE, Q_SCALE * tab)
        for hh in range(MLA_HEADS):
            sl = slice(hh * HEAD_LANES, (hh + 1) * HEAD_LANES)
            q_ref[0, r0:r1, sl] = (q[:, sl] * q_tab).astype(BF16)

        ckv = (_rms(proj[:, c0:c0 + KV_LORA_RANK]) * kvg_ref[...]).astype(BF16)
        knope = jnp.dot(ckv, wk_ref[...], preferred_element_type=F32)
        r = proj[:, c1:c1 + LANES] * tab
        kf = r + pltpu.roll(r, QK_ROPE_DIM, axis=1)
        for hp in range(MLA_HEADS // 2):
            pair = knope[:, hp * LANES:(hp + 1) * LANES]
            for hh, nope in ((2 * hp, pair), (2 * hp + 1, pltpu.roll(pair, QK_NOPE_DIM, axis=1))):
                sl = slice(hh * HEAD_LANES, (hh + 1) * HEAD_LANES)
                k_ref[0, r0:r1, sl] = jnp.where(is_nope, nope, kf).astype(BF16)
        vt_ref[0, :, r0:r1] = lax.dot_general(wvt_ref[...], ckv, (((1,), (1,)), ((), ())),
                                              preferred_element_type=F32).astype(BF16)

        gate_b = proj[:, c2:c2 + CONV_WIDTH]
        u = proj[:, c2 + CONV_WIDTH:c2 + 2 * CONV_WIDTH] * proj[:, c2 + 2 * CONV_WIDTH:c2 + 3 * CONV_WIDTH]
        cat = jnp.concatenate([prev, u], axis=0)
        u1 = pltpu.roll(cat, 1, axis=0)[8:]
        u2 = pltpu.roll(cat, 2, axis=0)[8:]
        prev = u[r1 - r0 - 8:]
        if r1 == t:
            carry_ref[...] = prev
        y = cb_ref[...] + cw_ref[0:1, :] * u2 + cw_ref[1:2, :] * u1 + cw_ref[2:3, :] * u
        y = gate_b * y
        for cc in range(CONV_WIDTH // LANES):
            sl = slice(cc * LANES, (cc + 1) * LANES)
            yv = y[:, sl]
            sq = yv * yv
            ms_lo = jnp.sum(jnp.where(lo, sq, 0.0), axis=-1, keepdims=True)
            ms_hi = jnp.sum(jnp.where(lo, 0.0, sq), axis=-1, keepdims=True)
            ms = jnp.where(lo, ms_lo, ms_hi) * (1.0 / (CONV_WIDTH // CONV_GROUPS))
            yc_ref[0, r0:r1, sl] = (yv * lax.rsqrt(ms + RMS_EPS) * og_ref[:, sl]).astype(BF16)


def _mixer_proj(x, mod, positions, win, qg, wq, kvg, wk, wvt, cw, cb, og):
    b, s, d = x.shape
    t = PROJ_TILE
    nt = s // t
    half = QK_ROPE_DIM // 2
    inv_freq = (ROPE_THETA ** (-jnp.arange(0, QK_ROPE_DIM, 2, dtype=F32) / QK_ROPE_DIM)).reshape(half, 1)
    e = _rope_expansion()
    const = lambda shape: pl.BlockSpec(shape, lambda bi, si: (0,) * len(shape))
    return pl.pallas_call(
        _proj_kernel,
        grid=(b, nt),
        in_specs=[pl.BlockSpec((1, t, d), lambda bi, si: (bi, si, 0)),
                  pl.BlockSpec((1, 6, d), lambda bi, si: (bi, 0, 0)),
                  pl.BlockSpec((1, 1, t), lambda bi, si: (bi, 0, si)),
                  const(inv_freq.shape), const(e.shape),
                  const(win.shape), const(qg.shape), const(wq.shape), const(kvg.shape),
                  const(wk.shape), const(wvt.shape), const(cw.shape), const(cb.shape), const(og.shape)],
        out_specs=[pl.BlockSpec((1, t, MLA_HEADS * HEAD_LANES), lambda bi, si: (bi, si, 0)),
                   pl.BlockSpec((1, t, MLA_HEADS * HEAD_LANES), lambda bi, si: (bi, si, 0)),
                   pl.BlockSpec((1, ATTN_WIDTH, t), lambda bi, si: (bi, 0, si)),
                   pl.BlockSpec((1, t, CONV_WIDTH), lambda bi, si: (bi, si, 0))],
        out_shape=[jax.ShapeDtypeStruct((b, s, MLA_HEADS * HEAD_LANES), BF16),
                   jax.ShapeDtypeStruct((b, s, MLA_HEADS * HEAD_LANES), BF16),
                   jax.ShapeDtypeStruct((b, ATTN_WIDTH, s), BF16),
                   jax.ShapeDtypeStruct((b, s, CONV_WIDTH), BF16)],
        scratch_shapes=[pltpu.VMEM((8, CONV_WIDTH), F32)],
        compiler_params=pltpu.CompilerParams(dimension_semantics=("arbitrary", "arbitrary"),
                                             vmem_limit_bytes=56 * 1024 * 1024, flags=SCHED_FLAGS),
        name="mixer_proj",
    )(x, mod, positions.reshape(b, 1, s), inv_freq, e, win, qg, wq, kvg, wk, wvt, cw, cb, og)


def _attn_kernel(q_ref, k_ref, vt_ref, g_ref, o_ref, s_ref, ot_ref):
    seq = q_ref.shape[1]
    qry_chunk = lax.broadcasted_iota(jnp.int32, (1, Q_TILE), 1) // CHUNK
    ones = jnp.ones((16, K_TILE), BF16)
    dn = (((1,), (1,)), ((), ()))

    def scores(qi, hh):
        q0 = qi * Q_TILE
        hl = slice(hh * HEAD_LANES, (hh + 1) * HEAD_LANES)
        q = q_ref[0, q0:q0 + Q_TILE, hl]
        if qi > 0:
            s_ref[hh % 2, 0:q0, :] = lax.dot_general(k_ref[0, 0:q0, hl], q, dn, preferred_element_type=F32)
        for half in range(2):
            nk = (half + 1) * HALF
            sd = lax.dot_general(k_ref[0, q0:q0 + nk, hl], q[half * HALF:(half + 1) * HALF], dn,
                                 preferred_element_type=F32)
            cols = slice(half * HALF, (half + 1) * HALF)
            for kc in range(nk // CHUNK):
                rows = slice(kc * CHUNK, (kc + 1) * CHUNK)
                vis = qry_chunk[:, cols] >= kc
                s_ref[hh % 2, q0 + kc * CHUNK:q0 + (kc + 1) * CHUNK, cols] = jnp.where(vis, sd[rows], NEG)

    def softmax_pv(qi, hh):
        kmax = (qi + 1) * Q_TILE
        n_full = kmax // K_TILE - 1
        last = slice(n_full * K_TILE, kmax)
        right = slice(HALF, Q_TILE)
        m = None
        for j in range(n_full):
            mj = jnp.max(s_ref[hh % 2, j * K_TILE:(j + 1) * K_TILE, :], axis=0, keepdims=True)
            m = mj if m is None else jnp.maximum(m, mj)
        m_last = jnp.max(s_ref[hh % 2, last, right], axis=0, keepdims=True)
        m = jnp.concatenate([m[:, :HALF], jnp.maximum(m[:, right], m_last)], axis=1)
        o = None
        for j in range(n_full):
            rows = slice(j * K_TILE, (j + 1) * K_TILE)
            p = jnp.exp2(s_ref[hh % 2, rows, :] - m).astype(BF16)
            lhs = jnp.concatenate([vt_ref[0, hh * V_HEAD_DIM:(hh + 1) * V_HEAD_DIM, rows], ones], axis=0)
            oj = jnp.dot(lhs, p, preferred_element_type=F32)
            o = oj if o is None else o + oj
        p = jnp.exp2(s_ref[hh % 2, last, right] - m[:, right]).astype(BF16)
        lhs = jnp.concatenate([vt_ref[0, hh * V_HEAD_DIM:(hh + 1) * V_HEAD_DIM, last], ones], axis=0)
        o = jnp.concatenate([o[:, :HALF], o[:, right] + jnp.dot(lhs, p, preferred_element_type=F32)], axis=1)
        o = o[0:V_HEAD_DIM] / o[V_HEAD_DIM:V_HEAD_DIM + 1]
        ms = jnp.mean(o * o, axis=0, keepdims=True)
        ot_ref[hh * V_HEAD_DIM:(hh + 1) * V_HEAD_DIM, :] = o * lax.rsqrt(ms + RMS_EPS)

    chains = [(qi, hh) for qi in range(seq // Q_TILE) for hh in range(ATTN_HEADS)]
    scores(*chains[0])
    for n, (qi, hh) in enumerate(chains):
        if n + 1 < len(chains):
            scores(*chains[n + 1])
        softmax_pv(qi, hh)
        if hh == ATTN_HEADS - 1:
            o_ref[0, qi * Q_TILE:(qi + 1) * Q_TILE, :] = (ot_ref[...].T * g_ref[...]).astype(BF16)


def _attention(q, k, vt, g):
    b, s, _ = q.shape
    n_h = ATTN_HEADS
    return pl.pallas_call(
        _attn_kernel,
        grid=(b, MLA_HEADS // n_h),
        in_specs=[pl.BlockSpec((1, s, n_h * HEAD_LANES), lambda bi, pi: (bi, 0, pi)),
                  pl.BlockSpec((1, s, n_h * HEAD_LANES), lambda bi, pi: (bi, 0, pi)),
                  pl.BlockSpec((1, n_h * V_HEAD_DIM, s), lambda bi, pi: (bi, pi, 0)),
                  pl.BlockSpec((1, n_h * V_HEAD_DIM), lambda bi, pi: (0, pi))],
        out_specs=pl.BlockSpec((1, s, n_h * V_HEAD_DIM), lambda bi, pi: (bi, 0, pi)),
        out_shape=jax.ShapeDtypeStruct((b, s, ATTN_WIDTH), BF16),
        scratch_shapes=[pltpu.VMEM((2, s, Q_TILE), F32),
                        pltpu.VMEM((n_h * V_HEAD_DIM, Q_TILE), F32)],
        compiler_params=pltpu.CompilerParams(dimension_semantics=("arbitrary",) * 2,
                                             vmem_limit_bytes=48 * 1024 * 1024, flags=SCHED_FLAGS),
        name="attention",
    )(q, k, vt, g)


def _ffn_kernel(x_ref, ya_ref, yc_ref, mod_ref, woa_ref, woc_ref, ln1g_ref, ln1b_ref, wup_ref,
                fcw_ref, fcb_ref, wdn_ref, ln2g_ref, ln2b_ref, o_ref, carry_ref, act_ref):
    t = TOK_TILE
    si = pl.program_id(1)
    gate_m = mod_ref[0, 2:3, :]
    shift_f = mod_ref[0, 3:4, :]
    scale_f = mod_ref[0, 4:5, :]
    gate_f = mod_ref[0, 5:6, :]

    @pl.when(si == 0)
    def _():
        carry_ref[...] = jnp.zeros(carry_ref.shape, F32)

    blocks = [(r, r + t // FFN_ROW_BLOCKS) for r in range(0, t, t // FFN_ROW_BLOCKS)]
    x1s, hs = [], []
    for r0, r1 in blocks:
        mix = (jnp.dot(ya_ref[0, r0:r1, :], woa_ref[...], preferred_element_type=F32)
               + jnp.dot(yc_ref[0, r0:r1, :], woc_ref[...], preferred_element_type=F32))
        x1 = _layer_norm(DEEPNORM_ALPHA * x_ref[0, r0:r1, :] + gate_m * mix) * ln1g_ref[...] + ln1b_ref[...]
        x1s.append(x1)
        hs.append((_layer_norm(x1) * (1.0 + scale_f) + shift_f).astype(BF16))

    def conv_cols(cols, h, prev, last):
        u = jnp.dot(h, wup_ref[:, cols], preferred_element_type=F32)
        cat = jnp.concatenate([prev, u], axis=0)
        u1 = pltpu.roll(cat, 1, axis=0)[8:]
        u2 = pltpu.roll(cat, 2, axis=0)[8:]
        tail = u[u.shape[0] - 8:]
        if last:
            carry_ref[:, cols] = tail
        y = fcb_ref[:, cols] + fcw_ref[0:1, cols] * u2 + fcw_ref[1:2, cols] * u1 + fcw_ref[2:3, cols] * u
        return y, tail

    for c in range(N_FF_CHUNKS):
        gcols = slice(c * FF_CHUNK, (c + 1) * FF_CHUNK)
        vcols = slice(D_FF + c * FF_CHUNK, D_FF + (c + 1) * FF_CHUNK)
        gprev, vprev = carry_ref[:, gcols], carry_ref[:, vcols]
        for bi, ((r0, r1), h) in enumerate(zip(blocks, hs)):
            last = bi == len(blocks) - 1
            g, gprev = conv_cols(gcols, h, gprev, last)
            v, vprev = conv_cols(vcols, h, vprev, last)
            act_ref[r0:r1, c * FF_CHUNK:(c + 1) * FF_CHUNK] = (g * _sigmoid(g) * v).astype(BF16)

    for (r0, r1), x1 in zip(blocks, x1s):
        ff = jnp.dot(act_ref[r0:r1, :], wdn_ref[...], preferred_element_type=F32)
        o_ref[0, r0:r1, :] = (_layer_norm(DEEPNORM_ALPHA * x1 + gate_f * ff) * ln2g_ref[...]
                              + ln2b_ref[...])


def _out_ffn(x, ya, yc, mod, woa, woc, ln1g, ln1b, wup, fcw, fcb, wdn, ln2g, ln2b):
    b, s, d = x.shape
    t = TOK_TILE
    const = lambda shape: pl.BlockSpec(shape, lambda bi, si: (0,) * len(shape),
                                       pipeline_mode=pl.Buffered(1))
    tok = lambda width: pl.BlockSpec((1, t, width), lambda bi, si: (bi, si, 0))
    return pl.pallas_call(
        _ffn_kernel,
        grid=(b, s // t),
        in_specs=[tok(d), tok(ATTN_WIDTH), tok(CONV_WIDTH),
                  pl.BlockSpec((1, 6, d), lambda bi, si: (bi, 0, 0)),
                  const(woa.shape), const(woc.shape), const(ln1g.shape), const(ln1b.shape),
                  const(wup.shape), const(fcw.shape), const(fcb.shape), const(wdn.shape),
                  const(ln2g.shape), const(ln2b.shape)],
        out_specs=tok(d),
        out_shape=jax.ShapeDtypeStruct((b, s, d), F32),
        scratch_shapes=[pltpu.VMEM((8, 2 * D_FF), F32),
                        pltpu.VMEM((t, D_FF), BF16)],
        compiler_params=pltpu.CompilerParams(dimension_semantics=("arbitrary", "arbitrary"),
                                             vmem_limit_bytes=56 * 1024 * 1024, flags=SCHED_FLAGS),
        name="out_ffn",
    )(x, ya, yc, mod, woa, woc, ln1g, ln1b, wup, fcw, fcb, wdn, ln2g, ln2b)


def _swap_halves(w):
    half = w.shape[-1] // 2
    return jnp.concatenate([w[..., half:], w[..., :half]], axis=-1)


def _prep_in_proj(w_in):
    c0 = Q_LORA_RANK + KV_LORA_RANK
    kr = w_in[:, c0:c0 + QK_ROPE_DIM]
    krs = _swap_halves(kr)
    return jnp.concatenate([w_in[:, :c0], kr, krs, kr, krs, w_in[:, c0 + QK_ROPE_DIM:]], axis=1).astype(BF16)


def _prep_q_up(w_q_up):
    w = w_q_up.reshape(Q_LORA_RANK, MLA_HEADS, QK_NOPE_DIM + QK_ROPE_DIM)
    rope = w[..., QK_NOPE_DIM:]
    w = jnp.concatenate([w[..., :QK_NOPE_DIM], rope, _swap_halves(rope)], axis=-1)
    return w.reshape(Q_LORA_RANK, MLA_HEADS * HEAD_LANES).astype(BF16)


def _prep_kv_up(w_kv_up):
    w = w_kv_up.reshape(KV_LORA_RANK, MLA_HEADS, QK_NOPE_DIM + V_HEAD_DIM)
    wk = w[..., :QK_NOPE_DIM].reshape(KV_LORA_RANK, MLA_HEADS * QK_NOPE_DIM)
    wvt = w[..., QK_NOPE_DIM:].reshape(KV_LORA_RANK, ATTN_WIDTH).T
    return wk.astype(BF16), wvt.astype(BF16)


def kernel(x, c, positions, w_ada, b_ada, w_in, q_norm_g, w_q_up, kv_norm_g, w_kv_up, conv_w, conv_b,
           out_norm_g, w_out, ln1_g, ln1_b, w_up, ffn_conv_w, ffn_conv_b, w_down, ln2_g, ln2_b):
    b, s, d = x.shape
    depth = w_ada.shape[0]
    for l in range(depth):
        mod = _adaln_mod(c, w_ada[l], b_ada[l]).reshape(b, 6, d)
        wk, wvt = _prep_kv_up(w_kv_up[l])
        q, k, vt, yc = _mixer_proj(
            x, mod, positions, _prep_in_proj(w_in[l]), q_norm_g[l].reshape(1, -1), _prep_q_up(w_q_up[l]),
            kv_norm_g[l].reshape(1, -1), wk, wvt, conv_w[l], conv_b[l].reshape(1, -1),
            out_norm_g[l, ATTN_WIDTH:].reshape(1, -1))
        ya = _attention(q, k, vt, out_norm_g[l, :ATTN_WIDTH].reshape(1, -1))
        x = _out_ffn(
            x, ya, yc, mod, w_out[l, :ATTN_WIDTH].astype(BF16), w_out[l, ATTN_WIDTH:].astype(BF16),
            ln1_g[l].reshape(1, -1), ln1_b[l].reshape(1, -1), w_up[l].astype(BF16),
            ffn_conv_w[l], ffn_conv_b[l].reshape(1, -1),
            w_down[l].astype(BF16), ln2_g[l].reshape(1, -1), ln2_b[l].reshape(1, -1))
    return x
```
